```python
import jax
import jax.numpy as jnp
from jax import lax
import numpy as np

D_MODEL = 2048
BATCH = 4
SEQ = 2048
DEPTH = 4

N_MEM = 256
EPS = 1e-6
NEG_BIG = -1e30
D_MIX = D_MODEL
GLA_HEADS = 4
GLA_DK = D_MIX // 32
GLA_DV = D_MIX // 16
GLA_RANK = 16
GLA_GATE_NORMALIZER = 16.0
HG_HEADS = 4
HG_DK = D_MIX // 16
HG_DV = D_MIX // 16
SWA_HEADS = 8
SWA_KV_HEADS = 2
SWA_HD = D_MIX // 16
SWA_WINDOW = 128
SWA_BLOCK = 128
ROPE_THETA = 500000.0
ROT_DIM = SWA_HD // 4
CHUNK = 64
SUB = 16
MEM_HEADS = 4
MEM_HD = 128
N_EXPERTS = 16
EC_CAPACITY = 2
D_EXPERT = 3 * D_MODEL // 4

GLA_QK = GLA_HEADS * GLA_DK
GLA_V = GLA_HEADS * GLA_DV
HG_K = HG_HEADS * HG_DK
HG_V = HG_HEADS * HG_DV
SWA_Q = SWA_HEADS * SWA_HD
SWA_KV = SWA_KV_HEADS * SWA_HD
MEM_W = MEM_HEADS * MEM_HD
IN_SIZES = (GLA_QK, GLA_QK, GLA_V, GLA_V, 2 * GLA_RANK,
            HG_K, HG_K, HG_K, HG_V, HG_V,
            SWA_Q, SWA_KV, SWA_KV)
IN_COLS = sum(IN_SIZES)

kernel_name = 'hybrid_parallel_heads_encoder'


def rmsnorm(t, w):
    tf = t.astype(jnp.float32)
    y = tf * lax.rsqrt(jnp.mean(tf * tf, axis=-1, keepdims=True) + EPS) * w.astype(jnp.float32)
    return y.astype(t.dtype)


def split_heads(t, h):
    b, s, _ = t.shape
    return t.reshape(b, s, h, -1).transpose(0, 2, 1, 3)


def merge_heads(t):
    b, h, s, d = t.shape
    return t.transpose(0, 2, 1, 3).reshape(b, s, h * d)


def rope_tables(s):
    inv = ROPE_THETA ** (-jnp.arange(0, ROT_DIM, 2, dtype=jnp.float32) / ROT_DIM)
    ang = jnp.arange(s, dtype=jnp.float32)[:, None] * inv[None, :]
    return jnp.cos(ang), jnp.sin(ang)


def apply_rope(t, cos, sin):
    half = ROT_DIM // 2
    t1, t2, rest = t[..., :half], t[..., half:ROT_DIM], t[..., ROT_DIM:]
    c, s = cos.astype(t.dtype), sin.astype(t.dtype)
    return jnp.concatenate([t1 * c - t2 * s, t2 * c + t1 * s, rest], axis=-1)


def chunk_gla(q, k, v, log_a):
    out_dtype = v.dtype
    q, k, v, log_a = (t.astype(jnp.float32) for t in (q, k, v, log_a))
    bsz, nh, seq, dk = q.shape
    dv = v.shape[-1]
    n = seq // CHUNK
    ns = CHUNK // SUB
    b_flat = jnp.cumsum(log_a.reshape(bsz, nh, n, CHUNK, dk), axis=3)
    k_flat = k.reshape(bsz, nh, n, CHUNK, dk)
    v_flat = v.reshape(bsz, nh, n, CHUNK, dv)
    b = b_flat.reshape(bsz, nh, n, ns, SUB, dk)
    q = q.reshape(bsz, nh, n, ns, SUB, dk)
    k = k.reshape(bsz, nh, n, ns, SUB, dk)
    v = v.reshape(bsz, nh, n, ns, SUB, dv)
    b_end = b_flat[:, :, :, -1]

    u = jnp.einsum('bhnsd,bhnse->bhnde', k_flat * jnp.exp(b_end[:, :, :, None] - b_flat), v_flat)

    def step(state, xs):
        decay, un = xs
        return decay[..., None] * state + un, state

    _, s_in = lax.scan(step, jnp.zeros((bsz, nh, dk, dv), jnp.float32),
                       (jnp.moveaxis(jnp.exp(b_end), 2, 0), jnp.moveaxis(u, 2, 0)))
    s_in = jnp.moveaxis(s_in, 0, 2)
    o_inter = jnp.einsum('bhnisd,bhnde->bhnise', q * jnp.exp(b), s_in)

    b_sub_end = b[:, :, :, :, -1]
    r = jnp.concatenate([jnp.zeros_like(b_sub_end[:, :, :, :1]), b_sub_end[:, :, :, :-1]], axis=3)
    q_rel = q * jnp.exp(b - r[:, :, :, :, None])
    earlier = (jnp.arange(CHUNK) // SUB)[None, :] < jnp.arange(ns)[:, None]
    expo = jnp.where(earlier[:, :, None], r[:, :, :, :, None] - b_flat[:, :, :, None], NEG_BIG)
    k_rel = k_flat[:, :, :, None] * jnp.exp(expo)
    a_off = jnp.einsum('bhnitd,bhnisd->bhnits', q_rel, k_rel)
    o_off = jnp.einsum('bhnits,bhnse->bhnite', a_off, v_flat)

    lower = jnp.arange(SUB)[:, None] >= jnp.arange(SUB)[None, :]
    expo_d = jnp.where(lower[:, :, None], b[..., :, None, :] - b[..., None, :, :], NEG_BIG)
    a_diag = jnp.sum(q[..., :, None, :] * k[..., None, :, :] * jnp.exp(expo_d), axis=-1)
    o_diag = jnp.einsum('bhnits,bhnise->bhnite', a_diag, v)

    return (o_inter + o_off + o_diag).reshape(bsz, nh, seq, dv).astype(out_dtype)


def bidirectional_gla(q, k_fwd, k_bwd, v, la_fwd, la_bwd):
    flip = lambda t: jnp.flip(t, axis=2)
    o_f = chunk_gla(q, k_fwd, v, la_fwd)
    o_b = flip(chunk_gla(flip(q), flip(k_bwd), flip(v), flip(la_bwd)))
    return o_f + o_b


def gla_group(q, k, v, g, lr, gk_w, gk_b, norm_w):
    bsz, s, _ = q.shape
    lr = lr.reshape(bsz, s, 2, GLA_RANK)
    gk = jnp.einsum('bsjr,jrk->bsjk', lr, gk_w) + gk_b
    log_a = jax.nn.log_sigmoid(gk.astype(jnp.float32)) / GLA_GATE_NORMALIZER
    qh = split_heads(q, GLA_HEADS) * (GLA_DK ** -0.5)
    kh = split_heads(k, GLA_HEADS)
    o = bidirectional_gla(qh, kh, kh, split_heads(v, GLA_HEADS),
                          split_heads(log_a[:, :, 0], GLA_HEADS), split_heads(log_a[:, :, 1], GLA_HEADS))
    return merge_heads(rmsnorm(o, norm_w)) * jax.nn.silu(g)


def hgrn2_group(q, f_fwd, f_bwd, v, g, lb, norm_w):
    def gates(f_raw, lb_dir):
        f = lb_dir + (1.0 - lb_dir) * jax.nn.sigmoid(f_raw.astype(jnp.float32))
        f = jnp.clip(f, 1e-6, 1.0)
        return split_heads(1.0 - f, HG_HEADS), split_heads(jnp.log(f), HG_HEADS)
    k_f, la_f = gates(f_fwd, lb[0])
    k_b, la_b = gates(f_bwd, lb[1])
    qh = split_heads(jax.nn.silu(q), HG_HEADS) * (HG_DK ** -0.5)
    o = bidirectional_gla(qh, k_f, k_b, split_heads(v, HG_HEADS), la_f, la_b)
    return merge_heads(rmsnorm(o, norm_w)) * jax.nn.silu(g)


def swa_group(q, k, v, sink, cos, sin):
    bsz, s, _ = q.shape
    nb = s // SWA_BLOCK
    grp = SWA_HEADS // SWA_KV_HEADS
    qh = apply_rope(split_heads(q, SWA_HEADS), cos, sin)
    kh = apply_rope(split_heads(k, SWA_KV_HEADS), cos, sin)
    vh = split_heads(v, SWA_KV_HEADS)
    qb = qh.reshape(bsz, SWA_KV_HEADS, grp, nb, SWA_BLOCK, SWA_HD)

    def band(t):
        tb = jnp.pad(t, ((0, 0), (0, 0), (SWA_BLOCK, SWA_BLOCK), (0, 0)))
        tb = tb.reshape(bsz, SWA_KV_HEADS, nb + 2, SWA_BLOCK, SWA_HD)
        return jnp.concatenate([tb[:, :, :-2], tb[:, :, 1:-1], tb[:, :, 2:]], axis=3)

    kb, vb = band(kh), band(vh)
    logits = jnp.einsum('bkgnqd,bknsd->bkgnqs', qb, kb).astype(jnp.float32) * (SWA_HD ** -0.5)
    qpos = jnp.arange(s).reshape(nb, SWA_BLOCK)
    kpos = (jnp.arange(nb)[:, None] - 1) * SWA_BLOCK + jnp.arange(3 * SWA_BLOCK)[None, :]
    kp = kpos[:, None, :]
    valid = (kp >= 0) & (kp < s) & (jnp.abs(kp - qpos[:, :, None]) <= SWA_WINDOW)
    logits = jnp.where(valid, logits, NEG_BIG)
    sink_col = jnp.broadcast_to(sink.astype(jnp.float32).reshape(SWA_KV_HEADS, grp, 1, 1, 1),
                                logits.shape[:-1] + (1,))
    p = jax.nn.softmax(jnp.concatenate([logits, sink_col], axis=-1), axis=-1)[..., :-1]
    o = jnp.einsum('bkgnqs,bknsd->bkgnqd', p.astype(vb.dtype), vb)
    return merge_heads(o.reshape(bsz, SWA_HEADS, s, SWA_HD))


def memory_cross_attention(xn, memn, wq, wkv, wo):
    q = split_heads(xn @ wq, MEM_HEADS)
    k, v = jnp.split(memn @ wkv, 2, axis=-1)
    k, v = split_heads(k, MEM_HEADS), split_heads(v, MEM_HEADS)
    logits = jnp.einsum('bhqd,bhkd->bhqk', q, k).astype(jnp.float32) * (MEM_HD ** -0.5)
    p = jax.nn.softmax(logits, axis=-1).astype(v.dtype)
    return merge_heads(jnp.einsum('bhqk,bhkd->bhqd', p, v)) @ wo


def expert_choice_ffn(xn, router_w, w_gate, w_up, w_down):
    bsz, s, _ = xn.shape
    cap = EC_CAPACITY * s // N_EXPERTS
    aff = jax.nn.softmax((xn @ router_w).astype(jnp.float32), axis=-1)
    gate, idx = lax.top_k(jnp.swapaxes(aff, 1, 2), cap)
    b_idx = jnp.arange(bsz)[:, None, None]
    xe = xn[b_idx, idx]
    hdn = jax.nn.silu(jnp.einsum('becd,edf->becf', xe, w_gate)) * jnp.einsum('becd,edf->becf', xe, w_up)
    y = jnp.einsum('becf,efd->becd', hdn, w_down) * gate[..., None].astype(xn.dtype)
    return jnp.zeros_like(xn).at[b_idx, idx].add(y)


def setup_inputs(seed: int = 0) -> dict:
    key = jax.random.key(seed)
    ks = jax.random.split(key, 24)

    def nrm(k, shape, scale):
        return jax.random.normal(k, shape, jnp.float32) * scale

    def gain(k, shape):
        return 1.0 + 0.02 * jax.random.normal(k, shape, jnp.float32)

    return {
        'x': nrm(ks[0], (BATCH, SEQ, D_MODEL), 1.0),
        'mem': nrm(ks[1], (BATCH, N_MEM, D_MODEL), 1.0),
        'norm_mix': gain(ks[2], (DEPTH, D_MODEL)),
        'w_in': nrm(ks[3], (DEPTH, D_MODEL, IN_COLS), D_MODEL ** -0.5),
        'gla_gk_w': nrm(ks[4], (DEPTH, 2, GLA_RANK, GLA_QK), GLA_RANK ** -0.5),
        'gla_gk_b': nrm(ks[5], (DEPTH, 2, GLA_QK), 0.1),
        'gla_norm': gain(ks[6], (DEPTH, GLA_DV)),
        'hgrn_lb_logits': nrm(ks[7], (DEPTH, 2, HG_K), 0.5),
        'hgrn_norm': gain(ks[8], (DEPTH, HG_DV)),
        'swa_sink': nrm(ks[9], (DEPTH, SWA_HEADS), 1.0),
        'w_out': nrm(ks[10], (DEPTH, D_MIX, D_MODEL), D_MIX ** -0.5),
        'norm_mem_q': gain(ks[11], (DEPTH, D_MODEL)),
        'norm_mem_kv': gain(ks[12], (DEPTH, D_MODEL)),
        'mem_wq': nrm(ks[13], (DEPTH, D_MODEL, MEM_W), D_MODEL ** -0.5),
        'mem_wkv': nrm(ks[14], (DEPTH, D_MODEL, 2 * MEM_W), D_MODEL ** -0.5),
        'mem_wo': nrm(ks[15], (DEPTH, MEM_W, D_MODEL), MEM_W ** -0.5),
        'norm_ffn': gain(ks[16], (DEPTH, D_MODEL)),
        'router_w': nrm(ks[17], (DEPTH, D_MODEL, N_EXPERTS), D_MODEL ** -0.5),
        'expert_w_gate': nrm(ks[18], (DEPTH, N_EXPERTS, D_MODEL, D_EXPERT), D_MODEL ** -0.5),
        'expert_w_up': nrm(ks[19], (DEPTH, N_EXPERTS, D_MODEL, D_EXPERT), D_MODEL ** -0.5),
        'expert_w_down': nrm(ks[20], (DEPTH, N_EXPERTS, D_EXPERT, D_MODEL), D_EXPERT ** -0.5),
        'norm_final': gain(ks[21], (D_MODEL,)),
    }


def reference(x, mem, norm_mix, w_in, gla_gk_w, gla_gk_b, gla_norm, hgrn_lb_logits, hgrn_norm,
              swa_sink, w_out, norm_mem_q, norm_mem_kv, mem_wq, mem_wkv, mem_wo, norm_ffn,
              router_w, expert_w_gate, expert_w_up, expert_w_down, norm_final):
    s = x.shape[1]
    cos, sin = rope_tables(s)
    sm = jax.nn.softmax(hgrn_lb_logits.astype(jnp.float32), axis=0)
    lower_bounds = jnp.clip(jnp.cumsum(sm, axis=0) - sm[0], 0.0, 1.0)
    offsets = np.cumsum(IN_SIZES)[:-1].tolist()
    for l in range(DEPTH):
        h = rmsnorm(x, norm_mix[l])
        (a_q, a_k, a_v, a_g, a_lr, b_q, b_ff, b_fb, b_v, b_g, c_q, c_k, c_v) = jnp.split(
            h @ w_in[l], offsets, axis=-1)
        o_a = gla_group(a_q, a_k, a_v, a_g, a_lr, gla_gk_w[l], gla_gk_b[l], gla_norm[l])
        o_b = hgrn2_group(b_q, b_ff, b_fb, b_v, b_g, lower_bounds[l], hgrn_norm[l])
        o_c = swa_group(c_q, c_k, c_v, swa_sink[l], cos, sin)
        x = x + jnp.concatenate([o_a, o_b, o_c], axis=-1) @ w_out[l]
        x = x + memory_cross_attention(rmsnorm(x, norm_mem_q[l]), rmsnorm(mem, norm_mem_kv[l]),
                                       mem_wq[l], mem_wkv[l], mem_wo[l])
        x = x + expert_choice_ffn(rmsnorm(x, norm_ffn[l]), router_w[l],
                                  expert_w_gate[l], expert_w_up[l], expert_w_down[l])
    return rmsnorm(x, norm_final)
```

```python
import functools

import numpy as np
import jax
import jax.numpy as jnp
from jax import lax
from jax.experimental import pallas as pl
from jax.experimental.pallas import tpu as pltpu

F32 = jnp.float32
BF16 = jnp.bfloat16

EPS = 1e-6
NEG_BIG = -1e30
LANES = 128

GLA_HEADS = 4
GLA_DK = 64
GLA_DV = 128
GLA_RANK = 16
GLA_GATE_NORMALIZER = 16.0
HG_HEADS = 4
HG_DK = 128
HG_DV = 128
SWA_HEADS = 8
SWA_KV_HEADS = 2
SWA_GROUP = SWA_HEADS // SWA_KV_HEADS
SWA_HD = 128
SWA_WINDOW = 128
SWA_BLOCK = 128
ROPE_THETA = 500000.0
ROT_DIM = SWA_HD // 4
MEM_HEADS = 4
MEM_HD = 128
N_EXPERTS = 16
EC_CAPACITY = 2

COL_AQ, COL_AK, COL_AV, COL_AG = 0, 256, 512, 1024
COL_BQ, COL_BFF, COL_BFB, COL_BV, COL_BG = 1536, 2048, 2560, 3072, 3584
COL_CQ, COL_CK, COL_CV = 4096, 5120, 5376
COL_LR = 5632
PROJ_COLS = 6144
LR_START, LR_END, IN_COLS = 1536, 1568, 5664

SCAN_CHUNK = 128
SCAN_LEVELS = 7
VMEM_LIMIT = 56 * 1024 * 1024

NT = (((1,), (1,)), ((), ()))
TN = (((0,), (0,)), ((), ()))


def _params(*sem):
    return pltpu.CompilerParams(dimension_semantics=sem, vmem_limit_bytes=VMEM_LIMIT)


def _rms(x, w):
    return x * lax.rsqrt(jnp.mean(x * x, axis=-1, keepdims=True) + EPS) * w


def _sigmoid(x):
    return 1.0 / (1.0 + jnp.exp(-x))


def _norm_matmul_kernel(x_ref, nw_ref, w_ref, o_ref, xn_ref):
    @pl.when(pl.program_id(1) == 0)
    def _():
        xn_ref[...] = _rms(x_ref[...], nw_ref[...]).astype(BF16)

    o_ref[...] = jnp.dot(xn_ref[...], w_ref[...], preferred_element_type=F32).astype(o_ref.dtype)


def norm_matmul(x, nw, w, tm, tn, out_dtype):
    m, k = x.shape
    n = w.shape[1]
    return pl.pallas_call(
        _norm_matmul_kernel,
        grid=(m // tm, n // tn),
        in_specs=[pl.BlockSpec((tm, k), lambda i, j: (i, 0)),
                  pl.BlockSpec((1, k), lambda i, j: (0, 0)),
                  pl.BlockSpec((k, tn), lambda i, j: (0, j))],
        out_specs=pl.BlockSpec((tm, tn), lambda i, j: (i, j)),
        out_shape=jax.ShapeDtypeStruct((m, n), out_dtype),
        scratch_shapes=[pltpu.VMEM((tm, k), BF16)],
        compiler_params=_params("parallel", "arbitrary"),
        name="norm_matmul",
    )(x, nw.reshape(1, k), w)


def _scan_constants(c, nlev):
    e = np.zeros((2, nlev + 2, c, c), np.float32)
    for l in range(nlev):
        m = 1 << l
        for i in range(c):
            start = (i // m) * m
            end = start + m - 1
            if (i // m) % 2 == 1:
                e[0, l, i, start:i + 1] = 1
                e[1, l, i, start:i] = 1
            else:
                e[0, l, i, i + 1:end + 1] = 1
                e[1, l, i, i:end + 1] = 1
    for i in range(c):
        e[0, nlev, i, :i + 1] = 1
        e[1, nlev, i, i:] = 1
        e[0, nlev + 1, i, i + 1:] = 1
        e[1, nlev + 1, i, :i] = 1
    t = np.arange(c)
    x = t[:, None] ^ t[None, :]
    lv = np.where(x > 0, np.floor(np.log2(np.maximum(x, 1))), nlev).astype(np.int32)
    lvf = np.where(t[:, None] >= t[None, :], lv, -1).astype(np.int32)
    return e.reshape(2, (nlev + 2) * c, c), np.stack([lvf, lvf.T])


def _scan_chunk(q, k, v, la, st_ref, e, lv, qmask, rev):
    c, nlev = SCAN_CHUNK, SCAN_LEVELS
    la_hi = la.astype(BF16)
    la_lo = (la - la_hi.astype(F32)).astype(BF16)
    x = (jnp.dot(e, la_hi, preferred_element_type=F32) + jnp.dot(e, la_lo, preferred_element_type=F32))
    f = jnp.exp(x)
    qm = q * qmask
    a = lax.dot_general(qm.astype(BF16), k.astype(BF16), NT, preferred_element_type=F32)
    a = jnp.where(lv == nlev, a, 0.0)
    for l in range(nlev):
        fl = f[l * c:(l + 1) * c]
        al = lax.dot_general((qm * fl).astype(BF16), (k * fl).astype(BF16), NT,
                             preferred_element_type=F32)
        a = jnp.where(lv == l, al, a)
    fq = f[nlev * c:(nlev + 1) * c]
    fk = f[(nlev + 1) * c:(nlev + 2) * c]
    st = st_ref[...]
    o = jnp.dot(a.astype(BF16), v, preferred_element_type=F32)
    o = o + lax.dot_general((qm * fq).astype(BF16), st.astype(BF16), NT, preferred_element_type=F32)
    decay = fq[0:1] if rev else fq[c - 1:c]
    st_ref[...] = st * decay + lax.dot_general(v, (k * fk).astype(BF16), TN,
                                               preferred_element_type=F32)
    return o


def _scan_driver(prologue, qmask, g_ref, nw_ref, e_ref, lv_ref, o_ref, of_ref, ob_ref, sf_ref, sb_ref):
    c = SCAN_CHUNK
    s = of_ref.shape[0]
    nc = s // c
    sf_ref[...] = jnp.zeros_like(sf_ref)
    sb_ref[...] = jnp.zeros_like(sb_ref)

    def body(n, carry):
        rf = pl.multiple_of(n * c, c)
        rb = pl.multiple_of((nc - 1 - n) * c, c)
        q, k, v, la = prologue(rf, 0)
        of_ref[pl.ds(rf, c), :] = _scan_chunk(q, k, v, la, sf_ref, e_ref[0], lv_ref[0], qmask, False)
        q, k, v, la = prologue(rb, 1)
        ob_ref[pl.ds(rb, c), :] = _scan_chunk(q, k, v, la, sb_ref, e_ref[1], lv_ref[1], qmask, True)
        return carry

    lax.fori_loop(0, nc, body, 0)

    def finish(n, carry):
        r = pl.multiple_of(n * c, c)
        o = of_ref[pl.ds(r, c), :] + ob_ref[pl.ds(r, c), :]
        g = g_ref[0, pl.ds(r, c), :]
        o_ref[0, pl.ds(r, c), :] = (_rms(o, nw_ref[...]) * (g * _sigmoid(g))).astype(o_ref.dtype)
        return carry

    lax.fori_loop(0, nc, finish, 0)


def _gla_kernel(q_ref, k_ref, v_ref, g_ref, lr_ref, gw_ref, gb_ref, nw_ref, e_ref, lv_ref,
                o_ref, of_ref, ob_ref, sf_ref, sb_ref):
    c = SCAN_CHUNK
    lane = lax.broadcasted_iota(jnp.int32, (1, LANES), 1)
    qmask = jnp.where(lane // GLA_DK == pl.program_id(1) % 2, GLA_DK ** -0.5, 0.0).astype(F32)

    def prologue(r, d):
        q = q_ref[0, pl.ds(r, c), :]
        k = k_ref[0, pl.ds(r, c), :]
        v = v_ref[0, pl.ds(r, c), :].astype(BF16)
        gk = jnp.dot(lr_ref[0, pl.ds(r, c), :].astype(BF16), gw_ref[d],
                     preferred_element_type=F32) + gb_ref[d]
        la = (jnp.minimum(gk, 0.0) - jnp.log(1.0 + jnp.exp(-jnp.abs(gk)))) * (1.0 / GLA_GATE_NORMALIZER)
        return q, k, v, la

    _scan_driver(prologue, qmask, g_ref, nw_ref, e_ref, lv_ref, o_ref, of_ref, ob_ref, sf_ref, sb_ref)


def _hgrn_kernel(q_ref, ff_ref, fb_ref, v_ref, g_ref, lb_ref, nw_ref, e_ref, lv_ref,
                 o_ref, of_ref, ob_ref, sf_ref, sb_ref):
    c = SCAN_CHUNK
    qmask = jnp.full((1, LANES), HG_DK ** -0.5, F32)

    def prologue(r, d):
        q = q_ref[0, pl.ds(r, c), :]
        q = q * _sigmoid(q)
        z = (fb_ref if d else ff_ref)[0, pl.ds(r, c), :]
        lb = lb_ref[d]
        f = jnp.clip(lb + (1.0 - lb) * _sigmoid(z), 1e-6, 1.0)
        v = v_ref[0, pl.ds(r, c), :].astype(BF16)
        return q, 1.0 - f, v, jnp.log(f)

    _scan_driver(prologue, qmask, g_ref, nw_ref, e_ref, lv_ref, o_ref, of_ref, ob_ref, sf_ref, sb_ref)


def _scan_call(kernel, name, proj, col_blocks, extra, extra_specs, norm_w, heads):
    b, s, _ = proj.shape
    c = SCAN_CHUNK
    e_np, lv_np = _scan_constants(c, SCAN_LEVELS)
    e = jnp.asarray(e_np, BF16)
    lv = jnp.asarray(lv_np, jnp.int32)

    def col_spec(fn):
        return pl.BlockSpec((1, s, LANES), lambda bi, h, fn=fn: (bi, 0, fn(h)))

    in_specs = [col_spec(fn) for fn in col_blocks] + extra_specs + [
        pl.BlockSpec((1, LANES), lambda bi, h: (0, 0)),
        pl.BlockSpec(e.shape, lambda bi, h: (0, 0, 0)),
        pl.BlockSpec(lv.shape, lambda bi, h: (0, 0, 0))]
    return pl.pallas_call(
        kernel,
        grid=(b, heads),
        in_specs=in_specs,
        out_specs=pl.BlockSpec((1, s, LANES), lambda bi, h: (bi, 0, h)),
        out_shape=jax.ShapeDtypeStruct((b, s, heads * LANES), BF16),
        scratch_shapes=[pltpu.VMEM((s, LANES), F32), pltpu.VMEM((s, LANES), F32),
                        pltpu.VMEM((LANES, LANES), F32), pltpu.VMEM((LANES, LANES), F32)],
        compiler_params=_params("parallel", "parallel"),
        name=name,
    )(*([proj] * len(col_blocks)), *extra, norm_w.reshape(1, LANES), e, lv)


def gla_group(proj, gk_w, gk_b, norm_w):
    gw = jnp.zeros((2, LANES, GLA_HEADS * GLA_DK), F32)
    gw = gw.at[0, 0:GLA_RANK].set(gk_w[0]).at[1, GLA_RANK:2 * GLA_RANK].set(gk_w[1]).astype(BF16)
    gb = gk_b.reshape(2, 1, GLA_HEADS * GLA_DK)
    cols = [lambda h: COL_AQ // LANES + h // 2, lambda h: COL_AK // LANES + h // 2,
            lambda h: COL_AV // LANES + h, lambda h: COL_AG // LANES + h,
            lambda h: COL_LR // LANES]
    extra_specs = [pl.BlockSpec((2, LANES, LANES), lambda bi, h: (0, 0, h // 2)),
                   pl.BlockSpec((2, 1, LANES), lambda bi, h: (0, 0, h // 2))]
    return _scan_call(_gla_kernel, "gla_scan", proj, cols, [gw, gb], extra_specs, norm_w, GLA_HEADS)


def hgrn_group(proj, lb, norm_w):
    cols = [lambda h: COL_BQ // LANES + h, lambda h: COL_BFF // LANES + h,
            lambda h: COL_BFB // LANES + h, lambda h: COL_BV // LANES + h,
            lambda h: COL_BG // LANES + h]
    extra_specs = [pl.BlockSpec((2, 1, LANES), lambda bi, h: (0, 0, h))]
    return _scan_call(_hgrn_kernel, "hgrn_scan", proj, cols, [lb.reshape(2, 1, HG_HEADS * HG_DK)],
                      extra_specs, norm_w, HG_HEADS)


def _rope(x, t):
    return (x * t[0] + pltpu.roll(x, LANES - ROT_DIM // 2, 1) * t[1]
            + pltpu.roll(x, ROT_DIM // 2, 1) * t[2])


def _swa_kernel(sink_ref, q_ref, k0_ref, k1_ref, k2_ref, v0_ref, v1_ref, v2_ref,
                tq_ref, t0_ref, t1_ref, t2_ref, o_ref):
    kv = pl.program_id(1)
    n = pl.program_id(2)
    nb = pl.num_programs(2)
    blk = SWA_BLOCK
    kb = jnp.concatenate([_rope(k0_ref[0], t0_ref), _rope(k1_ref[0], t1_ref), _rope(k2_ref[0], t2_ref)],
                         axis=0).astype(BF16)
    vb = jnp.concatenate([v0_ref[0], v1_ref[0], v2_ref[0]], axis=0).astype(BF16)
    qi = lax.broadcasted_iota(jnp.int32, (blk, 3 * blk), 0)
    ki = lax.broadcasted_iota(jnp.int32, (blk, 3 * blk), 1)
    kblock = n - 1 + ki // blk
    valid = (kblock >= 0) & (kblock < nb) & (jnp.abs(ki - blk - qi) <= SWA_WINDOW)
    for g in range(SWA_GROUP):
        qh = _rope(q_ref[0, :, g * SWA_HD:(g + 1) * SWA_HD], tq_ref).astype(BF16)
        logits = lax.dot_general(qh, kb, NT, preferred_element_type=F32) * (SWA_HD ** -0.5)
        logits = jnp.where(valid, logits, NEG_BIG)
        sink = sink_ref[0, kv * SWA_GROUP + g]
        m = jnp.maximum(jnp.max(logits, axis=-1, keepdims=True), sink)
        p = jnp.exp(logits - m)
        denom = jnp.sum(p, axis=-1, keepdims=True) + jnp.exp(sink - m)
        o = jnp.dot((p / denom).astype(BF16), vb, preferred_element_type=F32)
        o_ref[0, :, g * SWA_HD:(g + 1) * SWA_HD] = o.astype(o_ref.dtype)


def swa_group(proj, sink, rope_t):
    b, s, _ = proj.shape
    blk = SWA_BLOCK
    nb = s // blk
    gw = SWA_GROUP * SWA_HD

    def band(col, off):
        return pl.BlockSpec((1, blk, SWA_HD),
                            lambda bi, kv, n: (bi, jnp.clip(n + off, 0, nb - 1), col // SWA_HD + kv))

    def tab(off):
        return pl.BlockSpec((3, blk, LANES), lambda bi, kv, n: (0, jnp.clip(n + off, 0, nb - 1), 0))

    return pl.pallas_call(
        _swa_kernel,
        grid=(b, SWA_KV_HEADS, nb),
        in_specs=[pl.BlockSpec(memory_space=pltpu.SMEM),
                  pl.BlockSpec((1, blk, gw), lambda bi, kv, n: (bi, n, COL_CQ // gw + kv)),
                  band(COL_CK, -1), band(COL_CK, 0), band(COL_CK, 1),
                  band(COL_CV, -1), band(COL_CV, 0), band(COL_CV, 1),
                  tab(0), tab(-1), tab(0), tab(1)],
        out_specs=pl.BlockSpec((1, blk, gw), lambda bi, kv, n: (bi, n, kv)),
        out_shape=jax.ShapeDtypeStruct((b, s, SWA_HEADS * SWA_HD), BF16),
        compiler_params=_params("parallel", "parallel", "parallel"),
        name="swa",
    )(sink.reshape(1, SWA_HEADS), proj, proj, proj, proj, proj, proj, proj,
      rope_t, rope_t, rope_t, rope_t)


def rope_tables(s):
    half = ROT_DIM // 2
    inv = ROPE_THETA ** (-jnp.arange(0, ROT_DIM, 2, dtype=F32) / ROT_DIM)
    ang = jnp.arange(s, dtype=F32)[:, None] * inv[None, :]
    cos, sin = jnp.cos(ang), jnp.sin(ang)
    z = jnp.zeros((s, LANES - ROT_DIM), F32)
    zh = jnp.zeros((s, half), F32)
    return jnp.stack([jnp.concatenate([cos, cos, z + 1.0], axis=1),
                      jnp.concatenate([-sin, zh, z], axis=1),
                      jnp.concatenate([zh, sin, z], axis=1)])


def _outproj_kernel(oa_ref, ob_ref, oc_ref, wa_ref, wb_ref, wc_ref, x_ref, o_ref):
    acc = jnp.dot(oa_ref[...], wa_ref[...], preferred_element_type=F32)
    acc = acc + jnp.dot(ob_ref[...], wb_ref[...], preferred_element_type=F32)
    acc = acc + jnp.dot(oc_ref[...], wc_ref[...], preferred_element_type=F32)
    o_ref[...] = x_ref[...] + acc


def outproj(oa, ob, oc, w, x, tm, tn):
    m, d = x.shape
    ka, kb, kc = oa.shape[1], ob.shape[1], oc.shape[1]
    assert ka == kb and kc == ka + kb and w.shape == (ka + kb + kc, d)
    return pl.pallas_call(
        _outproj_kernel,
        grid=(m // tm, d // tn),
        in_specs=[pl.BlockSpec((tm, ka), lambda i, j: (i, 0)),
                  pl.BlockSpec((tm, kb), lambda i, j: (i, 0)),
                  pl.BlockSpec((tm, kc), lambda i, j: (i, 0)),
                  pl.BlockSpec((ka, tn), lambda i, j: (0, j)),
                  pl.BlockSpec((kb, tn), lambda i, j: (1, j)),
                  pl.BlockSpec((kc, tn), lambda i, j: (1, j)),
                  pl.BlockSpec((tm, tn), lambda i, j: (i, j))],
        out_specs=pl.BlockSpec((tm, tn), lambda i, j: (i, j)),
        out_shape=jax.ShapeDtypeStruct((m, d), F32),
        compiler_params=_params("parallel", "parallel"),
        name="outproj",
    )(oa, ob, oc, w, w, w, x)


def _memattn_kernel(x_ref, nw_ref, wq_ref, kv_ref, wo_ref, o_ref):
    x = x_ref[0]
    xn = _rms(x, nw_ref[...]).astype(BF16)
    q = jnp.dot(xn, wq_ref[...], preferred_element_type=F32).astype(BF16)
    w = MEM_HEADS * MEM_HD
    outs = []
    for h in range(MEM_HEADS):
        k = kv_ref[0, :, h * MEM_HD:(h + 1) * MEM_HD]
        v = kv_ref[0, :, w + h * MEM_HD:w + (h + 1) * MEM_HD]
        logits = lax.dot_general(q[:, h * MEM_HD:(h + 1) * MEM_HD], k, NT,
                                 preferred_element_type=F32) * (MEM_HD ** -0.5)
        p = jnp.exp(logits - jnp.max(logits, axis=-1, keepdims=True))
        p = p / jnp.sum(p, axis=-1, keepdims=True)
        outs.append(jnp.dot(p.astype(BF16), v, preferred_element_type=F32).astype(BF16))
    o = jnp.concatenate(outs, axis=-1)
    o_ref[0] = x + jnp.dot(o, wo_ref[...], preferred_element_type=F32)


def memattn(x, nw, wq, kv, wo, tm):
    b, s, d = x.shape
    n_mem, kvw = kv.shape[1], kv.shape[2]
    return pl.pallas_call(
        _memattn_kernel,
        grid=(b, s // tm),
        in_specs=[pl.BlockSpec((1, tm, d), lambda bi, i: (bi, i, 0)),
                  pl.BlockSpec((1, d), lambda bi, i: (0, 0)),
                  pl.BlockSpec(wq.shape, lambda bi, i: (0, 0)),
                  pl.BlockSpec((1, n_mem, kvw), lambda bi, i: (bi, 0, 0)),
                  pl.BlockSpec(wo.shape, lambda bi, i: (0, 0))],
        out_specs=pl.BlockSpec((1, tm, d), lambda bi, i: (bi, i, 0)),
        out_shape=jax.ShapeDtypeStruct((b, s, d), F32),
        compiler_params=_params("parallel", "parallel"),
        name="memattn",
    )(x, nw.reshape(1, d), wq, kv, wo)


def _router_kernel(x_ref, nw_ref, rw_ref, xn_ref, aff_ref):
    xn = _rms(x_ref[0], nw_ref[...]).astype(BF16)
    xn_ref[0] = xn
    logits = jnp.dot(xn, rw_ref[...], preferred_element_type=F32)
    lane = lax.broadcasted_iota(jnp.int32, logits.shape, 1)
    logits = jnp.where(lane < N_EXPERTS, logits, NEG_BIG)
    p = jnp.exp(logits - jnp.max(logits, axis=-1, keepdims=True))
    aff_ref[0] = p / jnp.sum(p, axis=-1, keepdims=True)


def router(x, nw, rw, tm):
    b, s, d = x.shape
    return pl.pallas_call(
        _router_kernel,
        grid=(b, s // tm),
        in_specs=[pl.BlockSpec((1, tm, d), lambda bi, i: (bi, i, 0)),
                  pl.BlockSpec((1, d), lambda bi, i: (0, 0)),
                  pl.BlockSpec((d, LANES), lambda bi, i: (0, 0))],
        out_specs=[pl.BlockSpec((1, tm, d), lambda bi, i: (bi, i, 0)),
                   pl.BlockSpec((1, tm, LANES), lambda bi, i: (bi, i, 0))],
        out_shape=[jax.ShapeDtypeStruct((b, s, d), BF16),
                   jax.ShapeDtypeStruct((b, s, LANES), F32)],
        compiler_params=_params("parallel", "parallel"),
        name="router",
    )(x, nw.reshape(1, d), rw)


def _select_kernel(aff_ref, tri_ref, slot_ref, *, cap):
    aff = aff_ref[...]
    key = pltpu.bitcast(aff, jnp.int32)

    def bit_step(i, lo):
        cand = lo | (1 << (30 - i))
        cnt = jnp.sum(jnp.where(key >= cand, 1.0, 0.0), axis=-1, keepdims=True)
        return jnp.where(cnt >= cap, cand, lo)

    thr = lax.fori_loop(0, 31, bit_step, jnp.zeros((aff.shape[0], 1), jnp.int32))
    gt = key > thr
    eq = key == thr
    need = cap - jnp.sum(jnp.where(gt, 1.0, 0.0), axis=-1, keepdims=True)
    tri = tri_ref[...]
    eq_rank = jnp.dot(jnp.where(eq, 1.0, 0.0).astype(BF16), tri, preferred_element_type=F32)
    sel = gt | (eq & (eq_rank <= need))
    pos = jnp.dot(jnp.where(sel, 1.0, 0.0).astype(BF16), tri, preferred_element_type=F32)
    slot_ref[...] = jnp.where(sel, pos.astype(jnp.int32) - 1, -1)


def select(aff_t, cap):
    r, s = aff_t.shape
    tri = jnp.triu(jnp.ones((s, s), BF16))
    return pl.pallas_call(
        functools.partial(_select_kernel, cap=cap),
        grid=(1,),
        in_specs=[pl.BlockSpec((r, s), lambda i: (0, 0)), pl.BlockSpec((s, s), lambda i: (0, 0))],
        out_specs=pl.BlockSpec((r, s), lambda i: (0, 0)),
        out_shape=jax.ShapeDtypeStruct((r, s), jnp.int32),
        compiler_params=_params("arbitrary"),
        name="select",
    )(aff_t, tri)


def _gather_kernel(slot_ref, aff_ref, xn_ref, xe_ref, gate_ref, *, cap):
    slot = slot_ref[0]
    hit = lax.broadcasted_iota(jnp.int32, (cap, slot.shape[1]), 0) == slot
    xe_ref[0, 0] = jnp.dot(jnp.where(hit, 1.0, 0.0).astype(BF16), xn_ref[0],
                           preferred_element_type=F32).astype(BF16)
    gate_ref[0, 0] = jnp.sum(jnp.where(hit, aff_ref[0], 0.0), axis=-1, keepdims=True)


def gather(slot_t, aff_t, xn, cap):
    b, s, d = xn.shape
    ne = slot_t.shape[0] // b
    return pl.pallas_call(
        functools.partial(_gather_kernel, cap=cap),
        grid=(b, ne),
        in_specs=[pl.BlockSpec((1, 1, s), lambda bi, e: (bi * ne + e, 0, 0)),
                  pl.BlockSpec((1, 1, s), lambda bi, e: (bi * ne + e, 0, 0)),
                  pl.BlockSpec((1, s, d), lambda bi, e: (bi, 0, 0))],
        out_specs=[pl.BlockSpec((1, 1, cap, d), lambda bi, e: (bi, e, 0, 0)),
                   pl.BlockSpec((1, 1, cap, 1), lambda bi, e: (bi, e, 0, 0))],
        out_shape=[jax.ShapeDtypeStruct((b, ne, cap, d), BF16),
                   jax.ShapeDtypeStruct((b, ne, cap, 1), F32)],
        compiler_params=_params("parallel", "arbitrary"),
        name="moe_gather",
    )(slot_t.reshape(b * ne, 1, s), aff_t.reshape(b * ne, 1, s), xn)


def _ffn_kernel(xe_ref, gate_ref, wg_ref, wu_ref, wd_ref, y_ref, acc_ref):
    f = pl.program_id(1)
    b, _, cap, d = xe_ref.shape
    xe = xe_ref[...].reshape(b * cap, d)
    hg = jnp.dot(xe, wg_ref[0].astype(BF16), preferred_element_type=F32)
    hu = jnp.dot(xe, wu_ref[0].astype(BF16), preferred_element_type=F32)
    h = hg * _sigmoid(hg) * hu
    y = jnp.dot(h.astype(BF16), wd_ref[0].astype(BF16), preferred_element_type=F32)

    @pl.when(f == 0)
    def _():
        acc_ref[...] = y

    @pl.when(f > 0)
    def _():
        acc_ref[...] += y

    @pl.when(f == pl.num_programs(1) - 1)
    def _():
        y_ref[...] = (acc_ref[...] * gate_ref[...].reshape(b * cap, 1)).reshape(b, 1, cap, d).astype(y_ref.dtype)


def expert_ffn(xe, gate, wg, wu, wd, tf):
    b, ne, cap, d = xe.shape
    dexp = wg.shape[2]
    return pl.pallas_call(
        _ffn_kernel,
        grid=(ne, dexp // tf),
        in_specs=[pl.BlockSpec((b, 1, cap, d), lambda e, f: (0, e, 0, 0)),
                  pl.BlockSpec((b, 1, cap, 1), lambda e, f: (0, e, 0, 0)),
                  pl.BlockSpec((1, d, tf), lambda e, f: (e, 0, f)),
                  pl.BlockSpec((1, d, tf), lambda e, f: (e, 0, f)),
                  pl.BlockSpec((1, tf, d), lambda e, f: (e, f, 0))],
        out_specs=pl.BlockSpec((b, 1, cap, d), lambda e, f: (0, e, 0, 0)),
        out_shape=jax.ShapeDtypeStruct((b, ne, cap, d), BF16),
        scratch_shapes=[pltpu.VMEM((b * cap, d), F32)],
        compiler_params=_params("parallel", "arbitrary"),
        name="expert_ffn",
    )(xe, gate, wg, wu, wd)


def _scatter_kernel(slot_ref, y_ref, x_ref, o_ref, *, cap):
    ne = y_ref.shape[1]
    s = x_ref.shape[1]
    lane = lax.broadcasted_iota(jnp.int32, (s, cap), 1)
    acc = x_ref[0]
    for e in range(ne):
        hit = slot_ref[0, :, e:e + 1] == lane
        acc = acc + jnp.dot(jnp.where(hit, 1.0, 0.0).astype(BF16), y_ref[0, e],
                            preferred_element_type=F32)
    o_ref[0] = acc


def scatter(slot, y, x, tn):
    b, s, d = x.shape
    ne, cap = y.shape[1], y.shape[2]
    return pl.pallas_call(
        functools.partial(_scatter_kernel, cap=cap),
        grid=(b, d // tn),
        in_specs=[pl.BlockSpec((1, s, ne), lambda bi, j: (bi, 0, 0)),
                  pl.BlockSpec((1, ne, cap, tn), lambda bi, j: (bi, 0, 0, j)),
                  pl.BlockSpec((1, s, tn), lambda bi, j: (bi, 0, j))],
        out_specs=pl.BlockSpec((1, s, tn), lambda bi, j: (bi, 0, j)),
        out_shape=jax.ShapeDtypeStruct((b, s, d), F32),
        compiler_params=_params("parallel", "parallel"),
        name="moe_scatter",
    )(slot, y, x)


def _norm_kernel(x_ref, nw_ref, o_ref):
    o_ref[...] = _rms(x_ref[...], nw_ref[...])


def final_norm(x, nw, tm):
    m, d = x.shape
    return pl.pallas_call(
        _norm_kernel,
        grid=(m // tm,),
        in_specs=[pl.BlockSpec((tm, d), lambda i: (i, 0)), pl.BlockSpec((1, d), lambda i: (0, 0))],
        out_specs=pl.BlockSpec((tm, d), lambda i: (i, 0)),
        out_shape=jax.ShapeDtypeStruct((m, d), F32),
        compiler_params=_params("parallel"),
        name="final_norm",
    )(x, nw.reshape(1, d))


def _reorder_w_in(w):
    d = w.shape[0]
    pad = jnp.zeros((d, PROJ_COLS - IN_COLS), w.dtype)
    return jnp.concatenate([w[:, :LR_START], w[:, LR_END:], w[:, LR_START:LR_END], pad], axis=1).astype(BF16)


def kernel(x, mem, norm_mix, w_in, gla_gk_w, gla_gk_b, gla_norm, hgrn_lb_logits, hgrn_norm, swa_sink, w_out, norm_mem_q, norm_mem_kv, mem_wq, mem_wkv, mem_wo, norm_ffn, router_w, expert_w_gate, expert_w_up, expert_w_down, norm_final):
    b, s, d = x.shape
    depth = w_in.shape[0]
    n_mem = mem.shape[1]
    cap = EC_CAPACITY * s // N_EXPERTS
    rope_t = rope_tables(s)
    sm = jax.nn.softmax(hgrn_lb_logits.astype(F32), axis=0)
    lower_bounds = jnp.clip(jnp.cumsum(sm, axis=0) - sm[0], 0.0, 1.0)
    mem2 = mem.reshape(b * n_mem, d)
    for l in range(depth):
        x2 = x.reshape(b * s, d)
        proj = norm_matmul(x2, norm_mix[l], _reorder_w_in(w_in[l]), 1024, 512, F32).reshape(b, s, PROJ_COLS)
        o_a = gla_group(proj, gla_gk_w[l], gla_gk_b[l], gla_norm[l])
        o_b = hgrn_group(proj, lower_bounds[l], hgrn_norm[l])
        o_c = swa_group(proj, swa_sink[l], rope_t)
        x2 = outproj(o_a.reshape(b * s, -1), o_b.reshape(b * s, -1), o_c.reshape(b * s, -1),
                     w_out[l].astype(BF16), x2, 1024, 512)
        kv = norm_matmul(mem2, norm_mem_kv[l], mem_wkv[l].astype(BF16), b * n_mem, 512, BF16)
        x = memattn(x2.reshape(b, s, d), norm_mem_q[l], mem_wq[l].astype(BF16),
                    kv.reshape(b, n_mem, -1), mem_wo[l].astype(BF16), 512)
        rw = jnp.pad(router_w[l], ((0, 0), (0, LANES - N_EXPERTS))).astype(BF16)
        xn, aff = router(x, norm_ffn[l], rw, 512)
        aff_t = jnp.swapaxes(aff[:, :, :N_EXPERTS], 1, 2).reshape(b * N_EXPERTS, s)
        slot_t = select(aff_t, cap)
        xe, gate = gather(slot_t, aff_t, xn, cap)
        y = expert_ffn(xe, gate, expert_w_gate[l], expert_w_up[l], expert_w_down[l], 256)
        slot = jnp.swapaxes(slot_t.reshape(b, N_EXPERTS, s), 1, 2)
        x = scatter(slot, y, x, 512)
    return final_norm(x.reshape(b * s, d), norm_final, 1024).reshape(b, s, d)
```

```python
import functools

import numpy as np
import jax
import jax.numpy as jnp
from jax import lax
from jax.experimental import pallas as pl
from jax.experimental.pallas import tpu as pltpu

F32 = jnp.float32
BF16 = jnp.bfloat16

EPS = 1e-6
NEG_BIG = -1e30
LANES = 128
SUBLANES = 8

GLA_HEADS = 4
GLA_DK = 64
GLA_DV = 128
GLA_RANK = 16
GLA_GATE_NORMALIZER = 16.0
HG_HEADS = 4
HG_DK = 128
HG_DV = 128
SWA_HEADS = 8
SWA_KV_HEADS = 2
SWA_GROUP = SWA_HEADS // SWA_KV_HEADS
SWA_HD = 128
SWA_WINDOW = 128
SWA_BLOCK = 128
SWA_Q_TILE = 512
ROPE_THETA = 500000.0
ROT_DIM = SWA_HD // 4
MEM_HEADS = 4
MEM_HD = 128
N_EXPERTS = 16
EC_CAPACITY = 2

COL_AQ, COL_AK, COL_AV, COL_AG = 0, 256, 512, 1024
COL_BQ, COL_BFF, COL_BFB, COL_BV, COL_BG = 1536, 2048, 2560, 3072, 3584
COL_CQ, COL_CK, COL_CV = 4096, 5120, 5376
COL_LR = 5632
PROJ_COLS = 6144
LR_START, LR_END, IN_COLS = 1536, 1568, 5664

SCAN_CHUNK = 128
SCAN_LEVELS = 7
SCAN_MM_LEVELS = 3
assert (1 << SCAN_LEVELS) == SCAN_CHUNK and (1 << SCAN_MM_LEVELS) == SUBLANES
VMEM_LIMIT = 56 * 1024 * 1024

NT = (((1,), (1,)), ((), ()))
TN = (((0,), (0,)), ((), ()))


def _params(*sem):
    return pltpu.CompilerParams(dimension_semantics=sem, vmem_limit_bytes=VMEM_LIMIT)


def _rms(x, w):
    return x * lax.rsqrt(jnp.mean(x * x, axis=-1, keepdims=True) + EPS) * w


def _sigmoid(x):
    return 0.5 * jnp.tanh(0.5 * x) + 0.5


def _norm_matmul_kernel(x_ref, nw_ref, w_ref, o_ref, xn_ref):
    @pl.when(pl.program_id(1) == 0)
    def _():
        xn_ref[...] = _rms(x_ref[...], nw_ref[...]).astype(BF16)

    o_ref[...] = jnp.dot(xn_ref[...], w_ref[...], preferred_element_type=F32).astype(o_ref.dtype)


def norm_matmul(x, nw, w, layer, tm, tn, out_dtype):
    m, k = x.shape
    n = w.shape[2]
    return pl.pallas_call(
        _norm_matmul_kernel,
        grid=(m // tm, n // tn),
        in_specs=[pl.BlockSpec((tm, k), lambda i, j: (i, 0)),
                  pl.BlockSpec((1, k), lambda i, j: (0, 0)),
                  pl.BlockSpec((None, k, tn), lambda i, j: (layer, 0, j))],
        out_specs=pl.BlockSpec((tm, tn), lambda i, j: (i, j)),
        out_shape=jax.ShapeDtypeStruct((m, n), out_dtype),
        scratch_shapes=[pltpu.VMEM((tm, k), BF16)],
        compiler_params=_params("parallel", "arbitrary"),
        name="norm_matmul",
    )(x, nw.reshape(1, k), w)


def _scan_constants(c, nlev, nmm):
    e = np.zeros((2, nmm + 1, c, c), np.float32)
    for i in range(c):
        e[0, 0, i, :i + 1] = 1
        e[1, 0, i, i:] = 1
    for l in range(nmm):
        m = 1 << l
        for i in range(c):
            start = (i // m) * m
            end = start + m - 1
            if (i // m) % 2 == 1:
                e[0, l + 1, i, start:i + 1] = 1
                e[1, l + 1, i, start:i] = 1
            else:
                e[0, l + 1, i, i + 1:end + 1] = 1
                e[1, l + 1, i, i:end + 1] = 1
    t = np.arange(c)
    x = t[:, None] ^ t[None, :]
    lv = np.where(x > 0, np.floor(np.log2(np.maximum(x, 1))), nlev).astype(np.int32)
    lvf = np.where(t[:, None] >= t[None, :], lv, -1).astype(np.int32)
    return e.reshape(2, (nmm + 1) * c, c), np.stack([lvf, lvf.T])


def _level_sums(b, l, rev):
    c = b.shape[0]
    m = 1 << l
    parts = []
    for j in range(c // (2 * m)):
        lo = 2 * m * j
        left, right = b[lo:lo + m], b[lo + m:lo + 2 * m]
        if rev:
            r = b[lo + m:lo + m + 1]
            parts += [left - r, r - right]
        else:
            r = b[lo + m - 1:lo + m]
            parts += [r - left, right - r]
    return jnp.concatenate(parts, axis=0)


def _scan_chunk(q, k, vs, la, st_refs, e, lv, qmasks, rev):
    c, nlev, nmm = SCAN_CHUNK, SCAN_LEVELS, SCAN_MM_LEVELS
    la_hi = la.astype(BF16)
    la_lo = (la - la_hi.astype(F32)).astype(BF16)
    y = jnp.dot(e, jnp.concatenate([la_hi, la_lo], axis=1), preferred_element_type=F32)
    x = y[:, :LANES] + y[:, LANES:]
    b = x[0:c]
    tot = b[0:1] if rev else b[c - 1:c]
    qms = [(q * qm).astype(BF16) for qm in qmasks]
    kb = k.astype(BF16)
    acc = [jnp.where(lv == nlev, lax.dot_general(qm, kb, NT, preferred_element_type=F32), 0.0)
           for qm in qms]
    for l in range(nlev):
        xl = x[(l + 1) * c:(l + 2) * c] if l < nmm else _level_sums(b, l, rev)
        fl = jnp.exp(xl).astype(BF16)
        ksl = kb * fl
        for h, qm in enumerate(qms):
            al = lax.dot_general(qm * fl, ksl, NT, preferred_element_type=F32)
            acc[h] = jnp.where(lv == l, al, acc[h])
    fq = jnp.exp(b).astype(BF16)
    ksk = kb * jnp.exp(tot - b).astype(BF16)
    decay = jnp.exp(tot)
    outs = []
    for h, qm in enumerate(qms):
        st = st_refs[h][...]
        o = jnp.dot(acc[h].astype(BF16), vs[h], preferred_element_type=F32)
        o = o + lax.dot_general(qm * fq, st.astype(BF16), NT, preferred_element_type=F32)
        st_refs[h][...] = st * decay + lax.dot_general(vs[h], ksk, TN, preferred_element_type=F32)
        outs.append(o)
    return outs


def _scan_driver(prologue, qmasks, g_ref, nw_ref, e_ref, lv_ref, o_ref, of_ref, ob_ref, st_ref):
    c = SCAN_CHUNK
    s = of_ref.shape[0]
    nc = s // c
    nh = len(qmasks)
    st_ref[...] = jnp.zeros_like(st_ref)

    def one(r, d, dst_ref):
        q, k, v, la = prologue(r, d)
        vs = [v[:, h * LANES:(h + 1) * LANES] for h in range(nh)]
        outs = _scan_chunk(q, k, vs, la, [st_ref.at[d * nh + h] for h in range(nh)],
                           e_ref[d], lv_ref[d], qmasks, bool(d))
        for h in range(nh):
            dst_ref[pl.ds(r, c), h * LANES:(h + 1) * LANES] = outs[h]

    def body(n, carry):
        one(pl.multiple_of(n * c, c), 0, of_ref)
        one(pl.multiple_of((nc - 1 - n) * c, c), 1, ob_ref)
        return carry

    lax.fori_loop(0, nc, body, 0, unroll=2)

    def finish(n, carry):
        r = pl.multiple_of(n * c, c)
        for h in range(nh):
            cols = slice(h * LANES, (h + 1) * LANES)
            o = of_ref[pl.ds(r, c), cols] + ob_ref[pl.ds(r, c), cols]
            g = g_ref[0, pl.ds(r, c), cols]
            o_ref[0, pl.ds(r, c), cols] = (_rms(o, nw_ref[...]) * (g * _sigmoid(g))).astype(o_ref.dtype)
        return carry

    lax.fori_loop(0, nc, finish, 0)


def _gla_kernel(q_ref, k_ref, v_ref, g_ref, lr_ref, gw_ref, gb_ref, nw_ref, e_ref, lv_ref,
                o_ref, of_ref, ob_ref, st_ref):
    c = SCAN_CHUNK
    lane = lax.broadcasted_iota(jnp.int32, (1, LANES), 1)
    qmasks = [jnp.where(lane // GLA_DK == h, GLA_DK ** -0.5, 0.0).astype(F32)
              for h in range(LANES // GLA_DK)]

    def prologue(r, d):
        q = q_ref[0, pl.ds(r, c), :]
        k = k_ref[0, pl.ds(r, c), :]
        v = v_ref[0, pl.ds(r, c), :].astype(BF16)
        gk = jnp.dot(lr_ref[0, pl.ds(r, c), :].astype(BF16), gw_ref[d],
                     preferred_element_type=F32) + gb_ref[d]
        la = (jnp.minimum(gk, 0.0) - jnp.log(1.0 + jnp.exp(-jnp.abs(gk)))) * (1.0 / GLA_GATE_NORMALIZER)
        return q, k, v, la

    _scan_driver(prologue, qmasks, g_ref, nw_ref, e_ref, lv_ref, o_ref, of_ref, ob_ref, st_ref)


def _hgrn_kernel(q_ref, ff_ref, fb_ref, v_ref, g_ref, lb_ref, nw_ref, e_ref, lv_ref,
                 o_ref, of_ref, ob_ref, st_ref):
    c = SCAN_CHUNK
    qmasks = [jnp.full((1, LANES), HG_DK ** -0.5, F32)]

    def prologue(r, d):
        q = q_ref[0, pl.ds(r, c), :]
        q = q * _sigmoid(q)
        z = (fb_ref if d else ff_ref)[0, pl.ds(r, c), :]
        lb = lb_ref[d]
        f = jnp.clip(lb + (1.0 - lb) * _sigmoid(z), 1e-6, 1.0)
        v = v_ref[0, pl.ds(r, c), :].astype(BF16)
        return q, 1.0 - f, v, jnp.log(f)

    _scan_driver(prologue, qmasks, g_ref, nw_ref, e_ref, lv_ref, o_ref, of_ref, ob_ref, st_ref)


def _scan_call(kernel, name, proj, cols, extra, extra_specs, norm_w, steps, heads_per_step):
    b, s, _ = proj.shape
    e_np, lv_np = _scan_constants(SCAN_CHUNK, SCAN_LEVELS, SCAN_MM_LEVELS)
    e = jnp.asarray(e_np, BF16)
    lv = jnp.asarray(lv_np, jnp.int32)
    ow = heads_per_step * LANES

    def col_spec(off, width):
        return pl.BlockSpec((1, s, width), lambda bi, h: (bi, 0, off // width + h))

    in_specs = [col_spec(off, width) for off, width in cols] + extra_specs + [
        pl.BlockSpec((1, LANES), lambda bi, h: (0, 0)),
        pl.BlockSpec(e.shape, lambda bi, h: (0, 0, 0)),
        pl.BlockSpec(lv.shape, lambda bi, h: (0, 0, 0))]
    return pl.pallas_call(
        kernel,
        grid=(b, steps),
        in_specs=in_specs,
        out_specs=pl.BlockSpec((1, s, ow), lambda bi, h: (bi, 0, h)),
        out_shape=jax.ShapeDtypeStruct((b, s, steps * ow), BF16),
        scratch_shapes=[pltpu.VMEM((s, ow), F32), pltpu.VMEM((s, ow), F32),
                        pltpu.VMEM((2 * heads_per_step, LANES, LANES), F32)],
        compiler_params=_params("parallel", "parallel"),
        name=name,
    )(*([proj] * len(cols)), *extra, norm_w.reshape(1, LANES), e, lv)


def gla_group(proj, gk_w, gk_b, norm_w):
    gw = jnp.zeros((2, LANES, GLA_HEADS * GLA_DK), F32)
    gw = gw.at[0, 0:GLA_RANK].set(gk_w[0]).at[1, GLA_RANK:2 * GLA_RANK].set(gk_w[1]).astype(BF16)
    gb = gk_b.reshape(2, 1, GLA_HEADS * GLA_DK)
    pair = LANES // GLA_DK
    cols = [(COL_AQ, LANES), (COL_AK, LANES), (COL_AV, pair * GLA_DV), (COL_AG, pair * GLA_DV)]
    extra_specs = [pl.BlockSpec((1, proj.shape[1], LANES), lambda bi, h: (bi, 0, COL_LR // LANES)),
                   pl.BlockSpec((2, LANES, LANES), lambda bi, h: (0, 0, h)),
                   pl.BlockSpec((2, 1, LANES), lambda bi, h: (0, 0, h))]
    return _scan_call(_gla_kernel, "gla_scan", proj, cols, [proj, gw, gb], extra_specs, norm_w,
                      GLA_HEADS // pair, pair)


def hgrn_group(proj, lb, norm_w):
    cols = [(COL_BQ, LANES), (COL_BFF, LANES), (COL_BFB, LANES), (COL_BV, LANES), (COL_BG, LANES)]
    extra_specs = [pl.BlockSpec((2, 1, LANES), lambda bi, h: (0, 0, h))]
    return _scan_call(_hgrn_kernel, "hgrn_scan", proj, cols, [lb.reshape(2, 1, HG_HEADS * HG_DK)],
                      extra_specs, norm_w, HG_HEADS, 1)


def _rope(x, cos, sin_hi, sin_lo):
    return (x * cos + pltpu.roll(x, LANES - ROT_DIM // 2, 1) * sin_hi
            + pltpu.roll(x, ROT_DIM // 2, 1) * sin_lo)


def _swa_kernel(sink_ref, q_ref, k_ref, v_ref, t_ref, o_ref, kr_ref, vb_ref):
    kv = pl.program_id(1)
    n = pl.program_id(2)
    blk = SWA_BLOCK
    s = k_ref.shape[1]
    nb = s // blk
    nq = q_ref.shape[1] // blk
    rows = SWA_GROUP * blk

    @pl.when(n == 0)
    def _():
        zero = jnp.zeros((blk, SWA_HD), BF16)
        for ref in (kr_ref, vb_ref):
            ref[0:blk, :] = zero
            ref[blk + s:2 * blk + s, :] = zero
        for j in range(nb):
            r = slice(j * blk, (j + 1) * blk)
            kr_ref[blk + j * blk:2 * blk + j * blk, :] = _rope(
                k_ref[0, r, :], t_ref[0, r, :], t_ref[1, r, :], t_ref[2, r, :]).astype(BF16)
            vb_ref[blk + j * blk:2 * blk + j * blk, :] = v_ref[0, r, :].astype(BF16)

    qi = lax.broadcasted_iota(jnp.int32, (rows, 3 * blk), 0) & (blk - 1)
    ki = lax.broadcasted_iota(jnp.int32, (rows, 3 * blk), 1)
    in_window = jnp.abs(ki - blk - qi) <= SWA_WINDOW
    head = lax.broadcasted_iota(jnp.int32, (rows, 1), 0) // blk
    sink = jnp.zeros((rows, 1), F32)
    for g in range(SWA_GROUP):
        sink = jnp.where(head == g, sink_ref[0, kv * SWA_GROUP + g], sink)
    for j in range(nq):
        qb = n * nq + j
        r0 = pl.multiple_of(qb * blk, blk)
        cos, sin_hi, sin_lo = (t_ref[i, pl.ds(r0, blk), :] for i in range(3))
        q4 = jnp.concatenate(
            [_rope(q_ref[0, j * blk:(j + 1) * blk, g * SWA_HD:(g + 1) * SWA_HD], cos, sin_hi, sin_lo)
             for g in range(SWA_GROUP)], axis=0).astype(BF16)
        kb = kr_ref[pl.ds(r0, 3 * blk), :]
        vb = vb_ref[pl.ds(r0, 3 * blk), :]
        logits = lax.dot_general(q4, kb, NT, preferred_element_type=F32) * (SWA_HD ** -0.5)
        kblock = qb - 1 + ki // blk
        logits = jnp.where(in_window & (kblock >= 0) & (kblock < nb), logits, NEG_BIG)
        m = jnp.maximum(jnp.max(logits, axis=-1, keepdims=True), sink)
        p = jnp.exp(logits - m)
        denom = jnp.sum(p, axis=-1, keepdims=True) + jnp.exp(sink - m)
        o = jnp.dot((p / denom).astype(BF16), vb, preferred_element_type=F32)
        for g in range(SWA_GROUP):
            o_ref[0, j * blk:(j + 1) * blk, g * SWA_HD:(g + 1) * SWA_HD] = (
                o[g * blk:(g + 1) * blk].astype(o_ref.dtype))


def swa_group(proj, sink, rope_t):
    b, s, _ = proj.shape
    gw = SWA_GROUP * SWA_HD
    tq = SWA_Q_TILE

    def kv_spec(col):
        return pl.BlockSpec((1, s, SWA_HD), lambda bi, kv, n: (bi, 0, col // SWA_HD + kv))

    return pl.pallas_call(
        _swa_kernel,
        grid=(b, SWA_KV_HEADS, s // tq),
        in_specs=[pl.BlockSpec(memory_space=pltpu.SMEM),
                  pl.BlockSpec((1, tq, gw), lambda bi, kv, n: (bi, n, COL_CQ // gw + kv)),
                  kv_spec(COL_CK), kv_spec(COL_CV),
                  pl.BlockSpec(rope_t.shape, lambda bi, kv, n: (0, 0, 0))],
        out_specs=pl.BlockSpec((1, tq, gw), lambda bi, kv, n: (bi, n, kv)),
        out_shape=jax.ShapeDtypeStruct((b, s, SWA_HEADS * SWA_HD), BF16),
        scratch_shapes=[pltpu.VMEM((s + 2 * SWA_BLOCK, SWA_HD), BF16),
                        pltpu.VMEM((s + 2 * SWA_BLOCK, SWA_HD), BF16)],
        compiler_params=_params("parallel", "parallel", "arbitrary"),
        name="swa",
    )(sink.reshape(1, SWA_HEADS), proj, proj, proj, rope_t)


def rope_tables(s):
    half = ROT_DIM // 2
    inv = ROPE_THETA ** (-jnp.arange(0, ROT_DIM, 2, dtype=F32) / ROT_DIM)
    ang = jnp.arange(s, dtype=F32)[:, None] * inv[None, :]
    cos, sin = jnp.cos(ang), jnp.sin(ang)
    z = jnp.zeros((s, LANES - ROT_DIM), F32)
    zh = jnp.zeros((s, half), F32)
    return jnp.stack([jnp.concatenate([cos, cos, z + 1.0], axis=1),
                      jnp.concatenate([-sin, zh, z], axis=1),
                      jnp.concatenate([zh, sin, z], axis=1)])


def _outproj_kernel(oa_ref, ob_ref, oc_ref, wa_ref, wb_ref, wc_ref, x_ref, o_ref):
    acc = jnp.dot(oa_ref[...], wa_ref[...], preferred_element_type=F32)
    acc = acc + jnp.dot(ob_ref[...], wb_ref[...], preferred_element_type=F32)
    acc = acc + jnp.dot(oc_ref[...], wc_ref[...], preferred_element_type=F32)
    o_ref[...] = x_ref[...] + acc


def outproj(oa, ob, oc, w, layer, x, tm, tn):
    m, d = x.shape
    ka, kb, kc = oa.shape[1], ob.shape[1], oc.shape[1]
    assert ka == kb and kc == ka + kb and w.shape[1:] == (ka + kb + kc, d)
    return pl.pallas_call(
        _outproj_kernel,
        grid=(m // tm, d // tn),
        in_specs=[pl.BlockSpec((tm, ka), lambda i, j: (i, 0)),
                  pl.BlockSpec((tm, kb), lambda i, j: (i, 0)),
                  pl.BlockSpec((tm, kc), lambda i, j: (i, 0)),
                  pl.BlockSpec((None, ka, tn), lambda i, j: (layer, 0, j)),
                  pl.BlockSpec((None, kb, tn), lambda i, j: (layer, 1, j)),
                  pl.BlockSpec((None, kc, tn), lambda i, j: (layer, 1, j)),
                  pl.BlockSpec((tm, tn), lambda i, j: (i, j))],
        out_specs=pl.BlockSpec((tm, tn), lambda i, j: (i, j)),
        out_shape=jax.ShapeDtypeStruct((m, d), F32),
        compiler_params=_params("parallel", "parallel"),
        name="outproj",
    )(oa, ob, oc, w, w, w, x)


def _memattn_kernel(x_ref, nw_ref, wq_ref, kv_ref, wo_ref, o_ref):
    x = x_ref[0]
    xn = _rms(x, nw_ref[...]).astype(BF16)
    q = jnp.dot(xn, wq_ref[...], preferred_element_type=F32).astype(BF16)
    w = MEM_HEADS * MEM_HD
    outs = []
    for h in range(MEM_HEADS):
        k = kv_ref[0, :, h * MEM_HD:(h + 1) * MEM_HD]
        v = kv_ref[0, :, w + h * MEM_HD:w + (h + 1) * MEM_HD]
        logits = lax.dot_general(q[:, h * MEM_HD:(h + 1) * MEM_HD], k, NT,
                                 preferred_element_type=F32) * (MEM_HD ** -0.5)
        p = jnp.exp(logits - jnp.max(logits, axis=-1, keepdims=True))
        p = p / jnp.sum(p, axis=-1, keepdims=True)
        outs.append(jnp.dot(p.astype(BF16), v, preferred_element_type=F32).astype(BF16))
    o = jnp.concatenate(outs, axis=-1)
    o_ref[0] = x + jnp.dot(o, wo_ref[...], preferred_element_type=F32)


def memattn(x, nw, wq, kv, wo, layer, tm):
    b, s, d = x.shape
    n_mem, kvw = kv.shape[1], kv.shape[2]
    return pl.pallas_call(
        _memattn_kernel,
        grid=(b, s // tm),
        in_specs=[pl.BlockSpec((1, tm, d), lambda bi, i: (bi, i, 0)),
                  pl.BlockSpec((1, d), lambda bi, i: (0, 0)),
                  pl.BlockSpec((None,) + wq.shape[1:], lambda bi, i: (layer, 0, 0)),
                  pl.BlockSpec((1, n_mem, kvw), lambda bi, i: (bi, 0, 0)),
                  pl.BlockSpec((None,) + wo.shape[1:], lambda bi, i: (layer, 0, 0))],
        out_specs=pl.BlockSpec((1, tm, d), lambda bi, i: (bi, i, 0)),
        out_shape=jax.ShapeDtypeStruct((b, s, d), F32),
        compiler_params=_params("parallel", "parallel"),
        name="memattn",
    )(x, nw.reshape(1, d), wq, kv, wo)


def _router_kernel(x_ref, nw_ref, rw_ref, xn_ref, aff_ref):
    xn = _rms(x_ref[0], nw_ref[...]).astype(BF16)
    xn_ref[0] = xn
    logits = jnp.dot(xn, rw_ref[...], preferred_element_type=F32)
    lane = lax.broadcasted_iota(jnp.int32, logits.shape, 1)
    logits = jnp.where(lane < N_EXPERTS, logits, NEG_BIG)
    p = jnp.exp(logits - jnp.max(logits, axis=-1, keepdims=True))
    aff_ref[0] = p / jnp.sum(p, axis=-1, keepdims=True)


def router(x, nw, rw, layer, tm):
    b, s, d = x.shape
    return pl.pallas_call(
        _router_kernel,
        grid=(b, s // tm),
        in_specs=[pl.BlockSpec((1, tm, d), lambda bi, i: (bi, i, 0)),
                  pl.BlockSpec((1, d), lambda bi, i: (0, 0)),
                  pl.BlockSpec((None, d, LANES), lambda bi, i: (layer, 0, 0))],
        out_specs=[pl.BlockSpec((1, tm, d), lambda bi, i: (bi, i, 0)),
                   pl.BlockSpec((1, tm, LANES), lambda bi, i: (bi, i, 0))],
        out_shape=[jax.ShapeDtypeStruct((b, s, d), BF16),
                   jax.ShapeDtypeStruct((b, s, LANES), F32)],
        compiler_params=_params("parallel", "parallel"),
        name="router",
    )(x, nw.reshape(1, d), rw)


def _select_kernel(aff_ref, tri_ref, slot_ref, *, cap):
    aff = aff_ref[...]
    key = pltpu.bitcast(aff, jnp.int32)

    def bit_step(i, lo):
        cand = lo | (1 << (30 - i))
        cnt = jnp.sum(jnp.where(key >= cand, 1.0, 0.0), axis=-1, keepdims=True)
        return jnp.where(cnt >= cap, cand, lo)

    thr = lax.fori_loop(0, 31, bit_step, jnp.zeros((aff.shape[0], 1), jnp.int32))
    gt = key > thr
    eq = key == thr
    need = cap - jnp.sum(jnp.where(gt, 1.0, 0.0), axis=-1, keepdims=True)
    tri = tri_ref[...]
    eq_rank = jnp.dot(jnp.where(eq, 1.0, 0.0).astype(BF16), tri, preferred_element_type=F32)
    sel = gt | (eq & (eq_rank <= need))
    pos = jnp.dot(jnp.where(sel, 1.0, 0.0).astype(BF16), tri, preferred_element_type=F32)
    slot_ref[...] = jnp.where(sel, pos.astype(jnp.int32) - 1, -1)


def select(aff_t, cap):
    r, s = aff_t.shape
    tri = jnp.triu(jnp.ones((s, s), BF16))
    return pl.pallas_call(
        functools.partial(_select_kernel, cap=cap),
        grid=(1,),
        in_specs=[pl.BlockSpec((r, s), lambda i: (0, 0)), pl.BlockSpec((s, s), lambda i: (0, 0))],
        out_specs=pl.BlockSpec((r, s), lambda i: (0, 0)),
        out_shape=jax.ShapeDtypeStruct((r, s), jnp.int32),
        compiler_params=_params("arbitrary"),
        name="select",
    )(aff_t, tri)


def _gather_kernel(slot_ref, aff_ref, xn_ref, xe_ref, gate_ref, *, cap):
    slot = slot_ref[0]
    hit = lax.broadcasted_iota(jnp.int32, (cap, slot.shape[1]), 0) == slot
    xe_ref[0, 0] = jnp.dot(jnp.where(hit, 1.0, 0.0).astype(BF16), xn_ref[0],
                           preferred_element_type=F32).astype(BF16)
    gate_ref[0, 0] = jnp.sum(jnp.where(hit, aff_ref[0], 0.0), axis=-1, keepdims=True)


def gather(slot_t, aff_t, xn, cap):
    b, s, d = xn.shape
    ne = slot_t.shape[0] // b
    return pl.pallas_call(
        functools.partial(_gather_kernel, cap=cap),
        grid=(b, ne),
        in_specs=[pl.BlockSpec((1, 1, s), lambda bi, e: (bi * ne + e, 0, 0)),
                  pl.BlockSpec((1, 1, s), lambda bi, e: (bi * ne + e, 0, 0)),
                  pl.BlockSpec((1, s, d), lambda bi, e: (bi, 0, 0))],
        out_specs=[pl.BlockSpec((1, 1, cap, d), lambda bi, e: (bi, e, 0, 0)),
                   pl.BlockSpec((1, 1, cap, 1), lambda bi, e: (bi, e, 0, 0))],
        out_shape=[jax.ShapeDtypeStruct((b, ne, cap, d), BF16),
                   jax.ShapeDtypeStruct((b, ne, cap, 1), F32)],
        compiler_params=_params("parallel", "arbitrary"),
        name="moe_gather",
    )(slot_t.reshape(b * ne, 1, s), aff_t.reshape(b * ne, 1, s), xn)


def _ffn_kernel(xe_ref, gate_ref, wg_ref, wu_ref, wd_ref, y_ref, acc_ref):
    f = pl.program_id(1)
    b, _, cap, d = xe_ref.shape
    xe = xe_ref[...].reshape(b * cap, d)
    hg = jnp.dot(xe, wg_ref[0].astype(BF16), preferred_element_type=F32)
    hu = jnp.dot(xe, wu_ref[0].astype(BF16), preferred_element_type=F32)
    h = hg * _sigmoid(hg) * hu
    y = jnp.dot(h.astype(BF16), wd_ref[0].astype(BF16), preferred_element_type=F32)

    @pl.when(f == 0)
    def _():
        acc_ref[...] = y

    @pl.when(f > 0)
    def _():
        acc_ref[...] += y

    @pl.when(f == pl.num_programs(1) - 1)
    def _():
        y_ref[...] = (acc_ref[...] * gate_ref[...].reshape(b * cap, 1)).reshape(b, 1, cap, d).astype(y_ref.dtype)


def expert_ffn(xe, gate, wg, wu, wd, layer, tf):
    b, ne, cap, d = xe.shape
    dexp = wg.shape[3]
    return pl.pallas_call(
        _ffn_kernel,
        grid=(ne, dexp // tf),
        in_specs=[pl.BlockSpec((b, 1, cap, d), lambda e, f: (0, e, 0, 0)),
                  pl.BlockSpec((b, 1, cap, 1), lambda e, f: (0, e, 0, 0)),
                  pl.BlockSpec((None, 1, d, tf), lambda e, f: (layer, e, 0, f)),
                  pl.BlockSpec((None, 1, d, tf), lambda e, f: (layer, e, 0, f)),
                  pl.BlockSpec((None, 1, tf, d), lambda e, f: (layer, e, f, 0))],
        out_specs=pl.BlockSpec((b, 1, cap, d), lambda e, f: (0, e, 0, 0)),
        out_shape=jax.ShapeDtypeStruct((b, ne, cap, d), BF16),
        scratch_shapes=[pltpu.VMEM((b * cap, d), F32)],
        compiler_params=_params("parallel", "arbitrary"),
        name="expert_ffn",
    )(xe, gate, wg, wu, wd)


def _scatter_kernel(slot_ref, y_ref, x_ref, o_ref, *, cap):
    ne = y_ref.shape[1]
    s = x_ref.shape[1]
    lane = lax.broadcasted_iota(jnp.int32, (s, cap), 1)
    acc = x_ref[0]
    for e in range(ne):
        hit = slot_ref[0, :, e:e + 1] == lane
        acc = acc + jnp.dot(jnp.where(hit, 1.0, 0.0).astype(BF16), y_ref[0, e],
                            preferred_element_type=F32)
    o_ref[0] = acc


def scatter(slot, y, x, tn):
    b, s, d = x.shape
    ne, cap = y.shape[1], y.shape[2]
    return pl.pallas_call(
        functools.partial(_scatter_kernel, cap=cap),
        grid=(b, d // tn),
        in_specs=[pl.BlockSpec((1, s, ne), lambda bi, j: (bi, 0, 0)),
                  pl.BlockSpec((1, ne, cap, tn), lambda bi, j: (bi, 0, 0, j)),
                  pl.BlockSpec((1, s, tn), lambda bi, j: (bi, 0, j))],
        out_specs=pl.BlockSpec((1, s, tn), lambda bi, j: (bi, 0, j)),
        out_shape=jax.ShapeDtypeStruct((b, s, d), F32),
        compiler_params=_params("parallel", "parallel"),
        name="moe_scatter",
    )(slot, y, x)


def _norm_kernel(x_ref, nw_ref, o_ref):
    o_ref[...] = _rms(x_ref[...], nw_ref[...])


def final_norm(x, nw, tm):
    m, d = x.shape
    return pl.pallas_call(
        _norm_kernel,
        grid=(m // tm,),
        in_specs=[pl.BlockSpec((tm, d), lambda i: (i, 0)), pl.BlockSpec((1, d), lambda i: (0, 0))],
        out_specs=pl.BlockSpec((tm, d), lambda i: (i, 0)),
        out_shape=jax.ShapeDtypeStruct((m, d), F32),
        compiler_params=_params("parallel"),
        name="final_norm",
    )(x, nw.reshape(1, d))


def reorder_w_in(w):
    pad = jnp.zeros(w.shape[:2] + (PROJ_COLS - IN_COLS,), w.dtype)
    return jnp.concatenate([w[..., :LR_START], w[..., LR_END:], w[..., LR_START:LR_END], pad],
                           axis=-1).astype(BF16)


def kernel(x, mem, norm_mix, w_in, gla_gk_w, gla_gk_b, gla_norm, hgrn_lb_logits, hgrn_norm, swa_sink, w_out, norm_mem_q, norm_mem_kv, mem_wq, mem_wkv, mem_wo, norm_ffn, router_w, expert_w_gate, expert_w_up, expert_w_down, norm_final):
    b, s, d = x.shape
    depth = w_in.shape[0]
    n_mem = mem.shape[1]
    cap = EC_CAPACITY * s // N_EXPERTS
    rope_t = rope_tables(s)
    sm = jax.nn.softmax(hgrn_lb_logits.astype(F32), axis=0)
    lower_bounds = jnp.clip(jnp.cumsum(sm, axis=0) - sm[0], 0.0, 1.0)
    mem2 = mem.reshape(b * n_mem, d)
    w_in_r = reorder_w_in(w_in)
    w_out_b, wq_b, wkv_b, wo_b = (w.astype(BF16) for w in (w_out, mem_wq, mem_wkv, mem_wo))
    rw_b = jnp.pad(router_w, ((0, 0), (0, 0), (0, LANES - N_EXPERTS))).astype(BF16)
    for l in range(depth):
        x2 = x.reshape(b * s, d)
        proj = norm_matmul(x2, norm_mix[l], w_in_r, l, 1024, 512, F32).reshape(b, s, PROJ_COLS)
        o_a = gla_group(proj, gla_gk_w[l], gla_gk_b[l], gla_norm[l])
        o_b = hgrn_group(proj, lower_bounds[l], hgrn_norm[l])
        o_c = swa_group(proj, swa_sink[l], rope_t)
        x2 = outproj(o_a.reshape(b * s, -1), o_b.reshape(b * s, -1), o_c.reshape(b * s, -1),
                     w_out_b, l, x2, 1024, 512)
        kv = norm_matmul(mem2, norm_mem_kv[l], wkv_b, l, b * n_mem, 512, BF16)
        x = memattn(x2.reshape(b, s, d), norm_mem_q[l], wq_b, kv.reshape(b, n_mem, -1), wo_b, l, 512)
        xn, aff = router(x, norm_ffn[l], rw_b, l, 512)
        aff_t = jnp.swapaxes(aff[:, :, :N_EXPERTS], 1, 2).reshape(b * N_EXPERTS, s)
        slot_t = select(aff_t, cap)
        xe, gate = gather(slot_t, aff_t, xn, cap)
        y = expert_ffn(xe, gate, expert_w_gate, expert_w_up, expert_w_down, l, 256)
        slot = jnp.swapaxes(slot_t.reshape(b, N_EXPERTS, s), 1, 2)
        x = scatter(slot, y, x, 512)
    return final_norm(x.reshape(b * s, d), norm_final, 1024).reshape(b, s, d)
```

```python
import functools

import numpy as np
import jax
import jax.numpy as jnp
from jax import lax
from jax.experimental import pallas as pl
from jax.experimental.pallas import tpu as pltpu

F32 = jnp.float32
BF16 = jnp.bfloat16

EPS = 1e-6
NEG_BIG = -1e30
LANES = 128
SUBLANES = 8

GLA_HEADS = 4
GLA_DK = 64
GLA_DV = 128
GLA_RANK = 16
GLA_GATE_NORMALIZER = 16.0
HG_HEADS = 4
HG_DK = 128
HG_DV = 128
SWA_HEADS = 8
SWA_KV_HEADS = 2
SWA_GROUP = SWA_HEADS // SWA_KV_HEADS
SWA_HD = 128
SWA_WINDOW = 128
SWA_BLOCK = 128
SWA_Q_TILE = 512
ROPE_THETA = 500000.0
ROT_DIM = SWA_HD // 4
MEM_HEADS = 4
MEM_HD = 128
N_EXPERTS = 16
EC_CAPACITY = 2

COL_AQ, COL_AK, COL_AV, COL_AG = 0, 256, 512, 1024
COL_BQ, COL_BFF, COL_BFB, COL_BV, COL_BG = 1536, 2048, 2560, 3072, 3584
COL_CQ, COL_CK, COL_CV = 4096, 5120, 5376
COL_LR = 5632
PROJ_COLS = 6144
LR_START, LR_END, IN_COLS = 1536, 1568, 5664

SCAN_CHUNK = 128
SCAN_LEVELS = 7
SCAN_MM_LEVELS = 3
assert (1 << SCAN_LEVELS) == SCAN_CHUNK and (1 << SCAN_MM_LEVELS) == SUBLANES
VMEM_LIMIT = 56 * 1024 * 1024

NT = (((1,), (1,)), ((), ()))
TN = (((0,), (0,)), ((), ()))


def _params(*sem):
    return pltpu.CompilerParams(dimension_semantics=sem, vmem_limit_bytes=VMEM_LIMIT)


def _rms(x, w):
    return x * lax.rsqrt(jnp.mean(x * x, axis=-1, keepdims=True) + EPS) * w


def _sigmoid(x):
    return 0.5 * jnp.tanh(0.5 * x) + 0.5


def _norm_matmul_kernel(x_ref, nw_ref, w_ref, o_ref, xn_ref):
    @pl.when(pl.program_id(1) == 0)
    def _():
        xn_ref[...] = _rms(x_ref[...], nw_ref[...]).astype(BF16)

    o_ref[...] = jnp.dot(xn_ref[...], w_ref[...], preferred_element_type=F32).astype(o_ref.dtype)


def norm_matmul(x, nw, w, layer, tm, tn, out_dtype):
    m, k = x.shape
    n = w.shape[2]
    return pl.pallas_call(
        _norm_matmul_kernel,
        grid=(m // tm, n // tn),
        in_specs=[pl.BlockSpec((tm, k), lambda i, j: (i, 0)),
                  pl.BlockSpec((1, k), lambda i, j: (0, 0)),
                  pl.BlockSpec((None, k, tn), lambda i, j: (layer, 0, j))],
        out_specs=pl.BlockSpec((tm, tn), lambda i, j: (i, j)),
        out_shape=jax.ShapeDtypeStruct((m, n), out_dtype),
        scratch_shapes=[pltpu.VMEM((tm, k), BF16)],
        compiler_params=_params("parallel", "arbitrary"),
        name="norm_matmul",
    )(x, nw.reshape(1, k), w)


def _scan_constants(c, nlev, nmm):
    e = np.zeros((2, nmm + 1, c, c), np.float32)
    for i in range(c):
        e[0, 0, i, :i + 1] = 1
        e[1, 0, i, i:] = 1
    for l in range(nmm):
        m = 1 << l
        for i in range(c):
            start = (i // m) * m
            end = start + m - 1
            if (i // m) % 2 == 1:
                e[0, l + 1, i, start:i + 1] = 1
                e[1, l + 1, i, start:i] = 1
            else:
                e[0, l + 1, i, i + 1:end + 1] = 1
                e[1, l + 1, i, i:end + 1] = 1
    t = np.arange(c)
    x = t[:, None] ^ t[None, :]
    lv = np.where(x > 0, np.floor(np.log2(np.maximum(x, 1))), nlev).astype(np.int32)
    lvf = np.where(t[:, None] >= t[None, :], lv, -1)
    masks = np.stack([[lvd == l for l in range(nlev + 1)] for lvd in (lvf, lvf.T)]).astype(np.float32)
    return e.reshape(2, (nmm + 1) * c, c), masks


def _level_sums(b, l, rev):
    c = b.shape[0]
    m = 1 << l
    parts = []
    for j in range(c // (2 * m)):
        lo = 2 * m * j
        left, right = b[lo:lo + m], b[lo + m:lo + 2 * m]
        if rev:
            r = b[lo + m:lo + m + 1]
            parts += [left - r, r - right]
        else:
            r = b[lo + m - 1:lo + m]
            parts += [r - left, right - r]
    return jnp.concatenate(parts, axis=0)


def _scan_chunk(q, k, vs, la, st_refs, e, mk_ref, qmasks, rev):
    c, nlev, nmm = SCAN_CHUNK, SCAN_LEVELS, SCAN_MM_LEVELS
    la_hi = la.astype(BF16)
    la_lo = (la - la_hi.astype(F32)).astype(BF16)
    y = jnp.dot(e, jnp.concatenate([la_hi, la_lo], axis=1), preferred_element_type=F32)
    x = y[:, :LANES] + y[:, LANES:]
    b = x[0:c]
    tot = b[0:1] if rev else b[c - 1:c]
    qms = [(q * qm).astype(BF16) for qm in qmasks]
    kb = k.astype(BF16)
    acc = [lax.dot_general(qm, kb, NT, preferred_element_type=F32).astype(BF16) * mk_ref[nlev]
           for qm in qms]
    for l in range(nlev):
        xl = x[(l + 1) * c:(l + 2) * c] if l < nmm else _level_sums(b, l, rev)
        fl = jnp.exp(xl).astype(BF16)
        ksl = kb * fl
        al = lax.dot_general(jnp.concatenate([qm * fl for qm in qms], axis=0), ksl, NT,
                             preferred_element_type=F32)
        for h in range(len(qms)):
            acc[h] = acc[h] + al[h * c:(h + 1) * c].astype(BF16) * mk_ref[l]
    fq = jnp.exp(b).astype(BF16)
    ksk = kb * jnp.exp(tot - b).astype(BF16)
    decay = jnp.exp(tot)
    outs = []
    for h, qm in enumerate(qms):
        st = st_refs[h][...]
        o = jnp.dot(acc[h], vs[h], preferred_element_type=F32)
        o = o + lax.dot_general(qm * fq, st.astype(BF16), NT, preferred_element_type=F32)
        st_refs[h][...] = st * decay + lax.dot_general(vs[h], ksk, TN, preferred_element_type=F32)
        outs.append(o)
    return outs


def _scan_driver(prologue, qmasks, g_ref, nw_ref, e_ref, mk_ref, o_ref, of_ref, ob_ref, st_ref):
    c = SCAN_CHUNK
    s = of_ref.shape[0]
    nc = s // c
    nh = len(qmasks)
    st_ref[...] = jnp.zeros_like(st_ref)

    def one(r, d, dst_ref):
        q, k, v, la = prologue(r, d)
        vs = [v[:, h * LANES:(h + 1) * LANES] for h in range(nh)]
        outs = _scan_chunk(q, k, vs, la, [st_ref.at[d * nh + h] for h in range(nh)],
                           e_ref[d], mk_ref.at[d], qmasks, bool(d))
        for h in range(nh):
            dst_ref[pl.ds(r, c), h * LANES:(h + 1) * LANES] = outs[h]

    def body(n, carry):
        one(pl.multiple_of(n * c, c), 0, of_ref)
        one(pl.multiple_of((nc - 1 - n) * c, c), 1, ob_ref)
        return carry

    lax.fori_loop(0, nc, body, 0, unroll=2)

    def finish(n, carry):
        r = pl.multiple_of(n * c, c)
        for h in range(nh):
            cols = slice(h * LANES, (h + 1) * LANES)
            o = of_ref[pl.ds(r, c), cols] + ob_ref[pl.ds(r, c), cols]
            g = g_ref[0, pl.ds(r, c), cols]
            o_ref[0, pl.ds(r, c), cols] = (_rms(o, nw_ref[...]) * (g * _sigmoid(g))).astype(o_ref.dtype)
        return carry

    lax.fori_loop(0, nc, finish, 0)


def _gla_kernel(q_ref, k_ref, v_ref, g_ref, lr_ref, gw_ref, gb_ref, nw_ref, e_ref, mk_ref,
                o_ref, of_ref, ob_ref, st_ref):
    c = SCAN_CHUNK
    lane = lax.broadcasted_iota(jnp.int32, (1, LANES), 1)
    qmasks = [jnp.where(lane // GLA_DK == h, GLA_DK ** -0.5, 0.0).astype(F32)
              for h in range(LANES // GLA_DK)]

    def prologue(r, d):
        q = q_ref[0, pl.ds(r, c), :]
        k = k_ref[0, pl.ds(r, c), :]
        v = v_ref[0, pl.ds(r, c), :].astype(BF16)
        gk = jnp.dot(lr_ref[0, pl.ds(r, c), :].astype(BF16), gw_ref[d],
                     preferred_element_type=F32) + gb_ref[d]
        la = (jnp.minimum(gk, 0.0) - jnp.log(1.0 + jnp.exp(-jnp.abs(gk)))) * (1.0 / GLA_GATE_NORMALIZER)
        return q, k, v, la

    _scan_driver(prologue, qmasks, g_ref, nw_ref, e_ref, mk_ref, o_ref, of_ref, ob_ref, st_ref)


def _hgrn_kernel(q_ref, ff_ref, fb_ref, v_ref, g_ref, lb_ref, nw_ref, e_ref, mk_ref,
                 o_ref, of_ref, ob_ref, st_ref):
    c = SCAN_CHUNK
    qmasks = [jnp.full((1, LANES), HG_DK ** -0.5, F32)]

    def prologue(r, d):
        q = q_ref[0, pl.ds(r, c), :]
        q = q * _sigmoid(q)
        z = (fb_ref if d else ff_ref)[0, pl.ds(r, c), :]
        lb = lb_ref[d]
        f = jnp.clip(lb + (1.0 - lb) * _sigmoid(z), 1e-6, 1.0)
        v = v_ref[0, pl.ds(r, c), :].astype(BF16)
        return q, 1.0 - f, v, jnp.log(f)

    _scan_driver(prologue, qmasks, g_ref, nw_ref, e_ref, mk_ref, o_ref, of_ref, ob_ref, st_ref)


def _scan_call(kernel, name, proj, cols, extra, extra_specs, norm_w, steps, heads_per_step):
    b, s, _ = proj.shape
    e_np, mk_np = _scan_constants(SCAN_CHUNK, SCAN_LEVELS, SCAN_MM_LEVELS)
    e = jnp.asarray(e_np, BF16)
    mk = jnp.asarray(mk_np, BF16)
    ow = heads_per_step * LANES

    def col_spec(off, width):
        return pl.BlockSpec((1, s, width), lambda bi, h: (bi, 0, off // width + h))

    in_specs = [col_spec(off, width) for off, width in cols] + extra_specs + [
        pl.BlockSpec((1, LANES), lambda bi, h: (0, 0)),
        pl.BlockSpec(e.shape, lambda bi, h: (0, 0, 0)),
        pl.BlockSpec(mk.shape, lambda bi, h: (0, 0, 0, 0))]
    return pl.pallas_call(
        kernel,
        grid=(b, steps),
        in_specs=in_specs,
        out_specs=pl.BlockSpec((1, s, ow), lambda bi, h: (bi, 0, h)),
        out_shape=jax.ShapeDtypeStruct((b, s, steps * ow), BF16),
        scratch_shapes=[pltpu.VMEM((s, ow), F32), pltpu.VMEM((s, ow), F32),
                        pltpu.VMEM((2 * heads_per_step, LANES, LANES), F32)],
        compiler_params=_params("parallel", "parallel"),
        name=name,
    )(*([proj] * len(cols)), *extra, norm_w.reshape(1, LANES), e, mk)


def gla_group(proj, gk_w, gk_b, norm_w):
    gw = jnp.zeros((2, LANES, GLA_HEADS * GLA_DK), F32)
    gw = gw.at[0, 0:GLA_RANK].set(gk_w[0]).at[1, GLA_RANK:2 * GLA_RANK].set(gk_w[1]).astype(BF16)
    gb = gk_b.reshape(2, 1, GLA_HEADS * GLA_DK)
    pair = LANES // GLA_DK
    cols = [(COL_AQ, LANES), (COL_AK, LANES), (COL_AV, pair * GLA_DV), (COL_AG, pair * GLA_DV)]
    extra_specs = [pl.BlockSpec((1, proj.shape[1], LANES), lambda bi, h: (bi, 0, COL_LR // LANES)),
                   pl.BlockSpec((2, LANES, LANES), lambda bi, h: (0, 0, h)),
                   pl.BlockSpec((2, 1, LANES), lambda bi, h: (0, 0, h))]
    return _scan_call(_gla_kernel, "gla_scan", proj, cols, [proj, gw, gb], extra_specs, norm_w,
                      GLA_HEADS // pair, pair)


def hgrn_group(proj, lb, norm_w):
    cols = [(COL_BQ, LANES), (COL_BFF, LANES), (COL_BFB, LANES), (COL_BV, LANES), (COL_BG, LANES)]
    extra_specs = [pl.BlockSpec((2, 1, LANES), lambda bi, h: (0, 0, h))]
    return _scan_call(_hgrn_kernel, "hgrn_scan", proj, cols, [lb.reshape(2, 1, HG_HEADS * HG_DK)],
                      extra_specs, norm_w, HG_HEADS, 1)


def _rope(x, cos, sin_hi, sin_lo):
    return (x * cos + pltpu.roll(x, LANES - ROT_DIM // 2, 1) * sin_hi
            + pltpu.roll(x, ROT_DIM // 2, 1) * sin_lo)


def _swa_kernel(sink_ref, q_ref, k_ref, v_ref, t_ref, o_ref, kr_ref, vb_ref):
    kv = pl.program_id(1)
    n = pl.program_id(2)
    blk = SWA_BLOCK
    s = k_ref.shape[1]
    nb = s // blk
    nq = q_ref.shape[1] // blk
    rows = SWA_GROUP * blk

    @pl.when(n == 0)
    def _():
        zero = jnp.zeros((blk, SWA_HD), BF16)
        for ref in (kr_ref, vb_ref):
            ref[0:blk, :] = zero
            ref[blk + s:2 * blk + s, :] = zero
        for j in range(nb):
            r = slice(j * blk, (j + 1) * blk)
            kr_ref[blk + j * blk:2 * blk + j * blk, :] = _rope(
                k_ref[0, r, :], t_ref[0, r, :], t_ref[1, r, :], t_ref[2, r, :]).astype(BF16)
            vb_ref[blk + j * blk:2 * blk + j * blk, :] = v_ref[0, r, :].astype(BF16)

    qi = lax.broadcasted_iota(jnp.int32, (rows, 3 * blk), 0) & (blk - 1)
    ki = lax.broadcasted_iota(jnp.int32, (rows, 3 * blk), 1)
    in_window = jnp.abs(ki - blk - qi) <= SWA_WINDOW
    head = lax.broadcasted_iota(jnp.int32, (rows, 1), 0) // blk
    sink = jnp.zeros((rows, 1), F32)
    for g in range(SWA_GROUP):
        sink = jnp.where(head == g, sink_ref[0, kv * SWA_GROUP + g], sink)
    for j in range(nq):
        qb = n * nq + j
        r0 = pl.multiple_of(qb * blk, blk)
        cos, sin_hi, sin_lo = (t_ref[i, pl.ds(r0, blk), :] for i in range(3))
        q4 = jnp.concatenate(
            [_rope(q_ref[0, j * blk:(j + 1) * blk, g * SWA_HD:(g + 1) * SWA_HD], cos, sin_hi, sin_lo)
             for g in range(SWA_GROUP)], axis=0).astype(BF16)
        kb = kr_ref[pl.ds(r0, 3 * blk), :]
        vb = vb_ref[pl.ds(r0, 3 * blk), :]
        logits = lax.dot_general(q4, kb, NT, preferred_element_type=F32) * (SWA_HD ** -0.5)
        kblock = qb - 1 + ki // blk
        logits = jnp.where(in_window & (kblock >= 0) & (kblock < nb), logits, NEG_BIG)
        m = jnp.maximum(jnp.max(logits, axis=-1, keepdims=True), sink)
        p = jnp.exp(logits - m)
        denom = jnp.sum(p, axis=-1, keepdims=True) + jnp.exp(sink - m)
        o = jnp.dot((p / denom).astype(BF16), vb, preferred_element_type=F32)
        for g in range(SWA_GROUP):
            o_ref[0, j * blk:(j + 1) * blk, g * SWA_HD:(g + 1) * SWA_HD] = (
                o[g * blk:(g + 1) * blk].astype(o_ref.dtype))


def swa_group(proj, sink, rope_t):
    b, s, _ = proj.shape
    gw = SWA_GROUP * SWA_HD
    tq = SWA_Q_TILE

    def kv_spec(col):
        return pl.BlockSpec((1, s, SWA_HD), lambda bi, kv, n: (bi, 0, col // SWA_HD + kv))

    return pl.pallas_call(
        _swa_kernel,
        grid=(b, SWA_KV_HEADS, s // tq),
        in_specs=[pl.BlockSpec(memory_space=pltpu.SMEM),
                  pl.BlockSpec((1, tq, gw), lambda bi, kv, n: (bi, n, COL_CQ // gw + kv)),
                  kv_spec(COL_CK), kv_spec(COL_CV),
                  pl.BlockSpec(rope_t.shape, lambda bi, kv, n: (0, 0, 0))],
        out_specs=pl.BlockSpec((1, tq, gw), lambda bi, kv, n: (bi, n, kv)),
        out_shape=jax.ShapeDtypeStruct((b, s, SWA_HEADS * SWA_HD), BF16),
        scratch_shapes=[pltpu.VMEM((s + 2 * SWA_BLOCK, SWA_HD), BF16),
                        pltpu.VMEM((s + 2 * SWA_BLOCK, SWA_HD), BF16)],
        compiler_params=_params("parallel", "parallel", "arbitrary"),
        name="swa",
    )(sink.reshape(1, SWA_HEADS), proj, proj, proj, rope_t)


def rope_tables(s):
    half = ROT_DIM // 2
    inv = ROPE_THETA ** (-jnp.arange(0, ROT_DIM, 2, dtype=F32) / ROT_DIM)
    ang = jnp.arange(s, dtype=F32)[:, None] * inv[None, :]
    cos, sin = jnp.cos(ang), jnp.sin(ang)
    z = jnp.zeros((s, LANES - ROT_DIM), F32)
    zh = jnp.zeros((s, half), F32)
    return jnp.stack([jnp.concatenate([cos, cos, z + 1.0], axis=1),
                      jnp.concatenate([-sin, zh, z], axis=1),
                      jnp.concatenate([zh, sin, z], axis=1)])


def _post_mixer_kernel(oa_ref, ob_ref, oc_ref, x_ref, wout_ref, nq_ref, wq_ref, kv_ref, wo_ref,
                       nf_ref, rw_ref, xo_ref, xn_ref, aff_ref):
    ka, kb = oa_ref.shape[-1], ob_ref.shape[-1]
    x = x_ref[0]
    x = x + jnp.dot(oa_ref[0], wout_ref[0:ka, :], preferred_element_type=F32)
    x = x + jnp.dot(ob_ref[0], wout_ref[ka:ka + kb, :], preferred_element_type=F32)
    x = x + jnp.dot(oc_ref[0], wout_ref[ka + kb:, :], preferred_element_type=F32)
    q = jnp.dot(_rms(x, nq_ref[...]).astype(BF16), wq_ref[...], preferred_element_type=F32).astype(BF16)
    w = MEM_HEADS * MEM_HD
    outs = []
    for h in range(MEM_HEADS):
        k = kv_ref[0, :, h * MEM_HD:(h + 1) * MEM_HD]
        v = kv_ref[0, :, w + h * MEM_HD:w + (h + 1) * MEM_HD]
        logits = lax.dot_general(q[:, h * MEM_HD:(h + 1) * MEM_HD], k, NT,
                                 preferred_element_type=F32) * (MEM_HD ** -0.5)
        p = jnp.exp(logits - jnp.max(logits, axis=-1, keepdims=True))
        p = p / jnp.sum(p, axis=-1, keepdims=True)
        outs.append(jnp.dot(p.astype(BF16), v, preferred_element_type=F32).astype(BF16))
    x = x + jnp.dot(jnp.concatenate(outs, axis=-1), wo_ref[...], preferred_element_type=F32)
    xo_ref[0] = x
    xn = _rms(x, nf_ref[...]).astype(BF16)
    xn_ref[0] = xn
    logits = jnp.dot(xn, rw_ref[...], preferred_element_type=F32)
    lane = lax.broadcasted_iota(jnp.int32, logits.shape, 1)
    logits = jnp.where(lane < N_EXPERTS, logits, NEG_BIG)
    p = jnp.exp(logits - jnp.max(logits, axis=-1, keepdims=True))
    aff_ref[0] = p / jnp.sum(p, axis=-1, keepdims=True)


def post_mixer(oa, ob, oc, x, w_out, nq, wq, kv, wo, nf, rw, layer, tm):
    b, s, d = x.shape
    n_mem, kvw = kv.shape[1], kv.shape[2]

    def rows(width):
        return pl.BlockSpec((1, tm, width), lambda bi, i: (bi, i, 0))

    def whole(a):
        return pl.BlockSpec((None,) + a.shape[1:], lambda bi, i: (layer, 0, 0),
                            pipeline_mode=pl.Buffered(1))

    vec = pl.BlockSpec((1, d), lambda bi, i: (0, 0))
    return pl.pallas_call(
        _post_mixer_kernel,
        grid=(b, s // tm),
        in_specs=[rows(oa.shape[2]), rows(ob.shape[2]), rows(oc.shape[2]), rows(d), whole(w_out),
                  vec, whole(wq), pl.BlockSpec((1, n_mem, kvw), lambda bi, i: (bi, 0, 0)), whole(wo),
                  vec, whole(rw)],
        out_specs=[rows(d), rows(d), rows(LANES)],
        out_shape=[jax.ShapeDtypeStruct((b, s, d), F32), jax.ShapeDtypeStruct((b, s, d), BF16),
                   jax.ShapeDtypeStruct((b, s, LANES), F32)],
        compiler_params=_params("parallel", "parallel"),
        name="post_mixer",
    )(oa, ob, oc, x, w_out, nq.reshape(1, d), wq, kv, wo, nf.reshape(1, d), rw)


def _select_kernel(aff_ref, tri_ref, slot_ref, *, cap):
    aff = aff_ref[...]
    key = pltpu.bitcast(aff, jnp.int32)

    def bit_step(i, lo):
        cand = lo | (1 << (30 - i))
        cnt = jnp.sum(jnp.where(key >= cand, 1.0, 0.0), axis=-1, keepdims=True)
        return jnp.where(cnt >= cap, cand, lo)

    thr = lax.fori_loop(0, 31, bit_step, jnp.zeros((aff.shape[0], 1), jnp.int32))
    gt = key > thr
    eq = key == thr
    need = cap - jnp.sum(jnp.where(gt, 1.0, 0.0), axis=-1, keepdims=True)
    tri = tri_ref[...]
    eq_rank = jnp.dot(jnp.where(eq, 1.0, 0.0).astype(BF16), tri, preferred_element_type=F32)
    sel = gt | (eq & (eq_rank <= need))
    pos = jnp.dot(jnp.where(sel, 1.0, 0.0).astype(BF16), tri, preferred_element_type=F32)
    slot_ref[...] = jnp.where(sel, pos.astype(jnp.int32) - 1, -1)


def select(aff_t, cap):
    r, s = aff_t.shape
    tri = jnp.triu(jnp.ones((s, s), BF16))
    return pl.pallas_call(
        functools.partial(_select_kernel, cap=cap),
        grid=(1,),
        in_specs=[pl.BlockSpec((r, s), lambda i: (0, 0)), pl.BlockSpec((s, s), lambda i: (0, 0))],
        out_specs=pl.BlockSpec((r, s), lambda i: (0, 0)),
        out_shape=jax.ShapeDtypeStruct((r, s), jnp.int32),
        compiler_params=_params("arbitrary"),
        name="select",
    )(aff_t, tri)


def _gather_kernel(slot_ref, aff_ref, xn_ref, xe_ref, gate_ref, *, cap):
    slot = slot_ref[0]
    hit = lax.broadcasted_iota(jnp.int32, (cap, slot.shape[1]), 0) == slot
    xe_ref[0, 0] = jnp.dot(jnp.where(hit, 1.0, 0.0).astype(BF16), xn_ref[0],
                           preferred_element_type=F32).astype(BF16)
    gate_ref[0, 0] = jnp.sum(jnp.where(hit, aff_ref[0], 0.0), axis=-1, keepdims=True)


def gather(slot_t, aff_t, xn, cap):
    b, s, d = xn.shape
    ne = slot_t.shape[0] // b
    return pl.pallas_call(
        functools.partial(_gather_kernel, cap=cap),
        grid=(b, ne),
        in_specs=[pl.BlockSpec((1, 1, s), lambda bi, e: (bi * ne + e, 0, 0)),
                  pl.BlockSpec((1, 1, s), lambda bi, e: (bi * ne + e, 0, 0)),
                  pl.BlockSpec((1, s, d), lambda bi, e: (bi, 0, 0))],
        out_specs=[pl.BlockSpec((1, 1, cap, d), lambda bi, e: (bi, e, 0, 0)),
                   pl.BlockSpec((1, 1, cap, 1), lambda bi, e: (bi, e, 0, 0))],
        out_shape=[jax.ShapeDtypeStruct((b, ne, cap, d), BF16),
                   jax.ShapeDtypeStruct((b, ne, cap, 1), F32)],
        compiler_params=_params("parallel", "arbitrary"),
        name="moe_gather",
    )(slot_t.reshape(b * ne, 1, s), aff_t.reshape(b * ne, 1, s), xn)


def _ffn_kernel(xe_ref, gate_ref, wg_ref, wu_ref, wd_ref, y_ref, h_ref):
    st = pl.program_id(1)
    nf, rows, tf = h_ref.shape
    d = xe_ref.shape[-1]

    @pl.when(st < nf)
    def _():
        xe = xe_ref[...].reshape(rows, d)
        hg = jnp.dot(xe, wg_ref[0].astype(BF16), preferred_element_type=F32)
        hu = jnp.dot(xe, wu_ref[0].astype(BF16), preferred_element_type=F32)
        h_ref[st] = (hg * _sigmoid(hg) * hu * gate_ref[...].reshape(rows, 1)).astype(BF16)

    @pl.when(st >= nf)
    def _():
        y = jnp.dot(h_ref[0], wd_ref[0, 0:tf, :].astype(BF16), preferred_element_type=F32)
        for f in range(1, nf):
            y = y + jnp.dot(h_ref[f], wd_ref[0, f * tf:(f + 1) * tf, :].astype(BF16),
                            preferred_element_type=F32)
        y_ref[...] = y.reshape(y_ref.shape).astype(y_ref.dtype)


def expert_ffn(xe, gate, wg, wu, wd, layer, tf, tn):
    b, ne, cap, d = xe.shape
    dexp = wg.shape[3]
    nf = dexp // tf
    return pl.pallas_call(
        _ffn_kernel,
        grid=(ne, nf + d // tn),
        in_specs=[pl.BlockSpec((b, 1, cap, d), lambda e, st: (0, e, 0, 0)),
                  pl.BlockSpec((b, 1, cap, 1), lambda e, st: (0, e, 0, 0)),
                  pl.BlockSpec((None, 1, d, tf), lambda e, st: (layer, e, 0, jnp.minimum(st, nf - 1))),
                  pl.BlockSpec((None, 1, d, tf), lambda e, st: (layer, e, 0, jnp.minimum(st, nf - 1))),
                  pl.BlockSpec((None, 1, dexp, tn), lambda e, st: (layer, e, 0, jnp.maximum(st - nf, 0)))],
        out_specs=pl.BlockSpec((b, 1, cap, tn), lambda e, st: (0, e, 0, jnp.maximum(st - nf, 0))),
        out_shape=jax.ShapeDtypeStruct((b, ne, cap, d), BF16),
        scratch_shapes=[pltpu.VMEM((nf, b * cap, tf), BF16)],
        compiler_params=_params("parallel", "arbitrary"),
        name="expert_ffn",
    )(xe, gate, wg, wu, wd)


def _scatter_kernel(slot_ref, y_ref, x_ref, o_ref, *, cap):
    ne = y_ref.shape[1]
    s = x_ref.shape[1]
    lane = lax.broadcasted_iota(jnp.int32, (s, cap), 1)
    acc = x_ref[0]
    for e in range(ne):
        hit = slot_ref[0, :, e:e + 1] == lane
        acc = acc + jnp.dot(jnp.where(hit, 1.0, 0.0).astype(BF16), y_ref[0, e],
                            preferred_element_type=F32)
    o_ref[0] = acc


def scatter(slot, y, x, tn):
    b, s, d = x.shape
    ne, cap = y.shape[1], y.shape[2]
    return pl.pallas_call(
        functools.partial(_scatter_kernel, cap=cap),
        grid=(b, d // tn),
        in_specs=[pl.BlockSpec((1, s, ne), lambda bi, j: (bi, 0, 0)),
                  pl.BlockSpec((1, ne, cap, tn), lambda bi, j: (bi, 0, 0, j)),
                  pl.BlockSpec((1, s, tn), lambda bi, j: (bi, 0, j))],
        out_specs=pl.BlockSpec((1, s, tn), lambda bi, j: (bi, 0, j)),
        out_shape=jax.ShapeDtypeStruct((b, s, d), F32),
        compiler_params=_params("parallel", "parallel"),
        name="moe_scatter",
    )(slot, y, x)


def _norm_kernel(x_ref, nw_ref, o_ref):
    o_ref[...] = _rms(x_ref[...], nw_ref[...])


def final_norm(x, nw, tm):
    m, d = x.shape
    return pl.pallas_call(
        _norm_kernel,
        grid=(m // tm,),
        in_specs=[pl.BlockSpec((tm, d), lambda i: (i, 0)), pl.BlockSpec((1, d), lambda i: (0, 0))],
        out_specs=pl.BlockSpec((tm, d), lambda i: (i, 0)),
        out_shape=jax.ShapeDtypeStruct((m, d), F32),
        compiler_params=_params("parallel"),
        name="final_norm",
    )(x, nw.reshape(1, d))


def _reorder_kernel(w_ref, o_ref):
    lr = LR_END - LR_START
    o_ref[:, 0:LR_START] = w_ref[:, 0:LR_START].astype(BF16)
    o_ref[:, LR_START:COL_LR] = w_ref[:, LR_END:IN_COLS].astype(BF16)
    o_ref[:, COL_LR:COL_LR + lr] = w_ref[:, LR_START:LR_END].astype(BF16)
    o_ref[:, COL_LR + lr:] = jnp.zeros((o_ref.shape[0], PROJ_COLS - COL_LR - lr), BF16)


def reorder_w_in(w, tr):
    nl, d, cols = w.shape
    assert cols == IN_COLS
    return pl.pallas_call(
        _reorder_kernel,
        grid=(nl, d // tr),
        in_specs=[pl.BlockSpec((None, tr, cols), lambda l, i: (l, i, 0))],
        out_specs=pl.BlockSpec((None, tr, PROJ_COLS), lambda l, i: (l, i, 0)),
        out_shape=jax.ShapeDtypeStruct((nl, d, PROJ_COLS), BF16),
        compiler_params=_params("parallel", "parallel"),
        name="reorder_w_in",
    )(w)


def kernel(x, mem, norm_mix, w_in, gla_gk_w, gla_gk_b, gla_norm, hgrn_lb_logits, hgrn_norm, swa_sink, w_out, norm_mem_q, norm_mem_kv, mem_wq, mem_wkv, mem_wo, norm_ffn, router_w, expert_w_gate, expert_w_up, expert_w_down, norm_final):
    b, s, d = x.shape
    depth = w_in.shape[0]
    n_mem = mem.shape[1]
    cap = EC_CAPACITY * s // N_EXPERTS
    rope_t = rope_tables(s)
    sm = jax.nn.softmax(hgrn_lb_logits.astype(F32), axis=0)
    lower_bounds = jnp.clip(jnp.cumsum(sm, axis=0) - sm[0], 0.0, 1.0)
    mem2 = mem.reshape(b * n_mem, d)
    w_in_r = reorder_w_in(w_in, 256)
    w_out_b, wq_b, wkv_b, wo_b = (w.astype(BF16) for w in (w_out, mem_wq, mem_wkv, mem_wo))
    rw_b = jnp.pad(router_w, ((0, 0), (0, 0), (0, LANES - N_EXPERTS))).astype(BF16)
    for l in range(depth):
        proj = norm_matmul(x.reshape(b * s, d), norm_mix[l], w_in_r, l, 1024, 1024, F32)
        proj = proj.reshape(b, s, PROJ_COLS)
        o_a = gla_group(proj, gla_gk_w[l], gla_gk_b[l], gla_norm[l])
        o_b = hgrn_group(proj, lower_bounds[l], hgrn_norm[l])
        o_c = swa_group(proj, swa_sink[l], rope_t)
        kv = norm_matmul(mem2, norm_mem_kv[l], wkv_b, l, b * n_mem, 512, BF16)
        x, xn, aff = post_mixer(o_a, o_b, o_c, x, w_out_b, norm_mem_q[l], wq_b,
                                kv.reshape(b, n_mem, -1), wo_b, norm_ffn[l], rw_b, l, 512)
        aff_t = jnp.swapaxes(aff[:, :, :N_EXPERTS], 1, 2).reshape(b * N_EXPERTS, s)
        slot_t = select(aff_t, cap)
        xe, gate = gather(slot_t, aff_t, xn, cap)
        y = expert_ffn(xe, gate, expert_w_gate, expert_w_up, expert_w_down, l, 512, 512)
        slot = jnp.swapaxes(slot_t.reshape(b, N_EXPERTS, s), 1, 2)
        x = scatter(slot, y, x, 512)
    return final_norm(x.reshape(b * s, d), norm_final, 1024).reshape(b, s, d)
```

```python
import functools

import numpy as np
import jax
import jax.numpy as jnp
from jax import lax
from jax.experimental import pallas as pl
from jax.experimental.pallas import tpu as pltpu

F32 = jnp.float32
BF16 = jnp.bfloat16

EPS = 1e-6
NEG_BIG = -1e30
LANES = 128
SUBLANES = 8

GLA_HEADS = 4
GLA_DK = 64
GLA_DV = 128
GLA_RANK = 16
GLA_GATE_NORMALIZER = 16.0
HG_HEADS = 4
HG_DK = 128
HG_DV = 128
SWA_HEADS = 8
SWA_KV_HEADS = 2
SWA_GROUP = SWA_HEADS // SWA_KV_HEADS
SWA_HD = 128
SWA_WINDOW = 128
SWA_BLOCK = 128
SWA_Q_TILE = 512
ROPE_THETA = 500000.0
ROT_DIM = SWA_HD // 4
MEM_HEADS = 4
MEM_HD = 128
N_EXPERTS = 16
EC_CAPACITY = 2

COL_AQ, COL_AK, COL_AV, COL_AG = 0, 256, 512, 1024
COL_BQ, COL_BFF, COL_BFB, COL_BV, COL_BG = 1536, 2048, 2560, 3072, 3584
COL_CQ, COL_CK, COL_CV = 4096, 5120, 5376
COL_LR = 5632
PROJ_COLS = 6144
LR_START, LR_END, IN_COLS = 1536, 1568, 5664

SCAN_CHUNK = 128
SCAN_LEVELS = 7
SCAN_MM_LEVELS = 3
assert (1 << SCAN_LEVELS) == SCAN_CHUNK and (1 << SCAN_MM_LEVELS) == SUBLANES
VMEM_LIMIT = 56 * 1024 * 1024

NT = (((1,), (1,)), ((), ()))
TN = (((0,), (0,)), ((), ()))


def _params(*sem):
    return pltpu.CompilerParams(dimension_semantics=sem, vmem_limit_bytes=VMEM_LIMIT)


def _rms(x, w):
    return x * lax.rsqrt(jnp.mean(x * x, axis=-1, keepdims=True) + EPS) * w


def _sigmoid(x):
    return 0.5 * jnp.tanh(0.5 * x) + 0.5


def _norm_matmul_kernel(x_ref, nw_ref, w_ref, o_ref, xn_ref, *, w_rows):
    @pl.when(pl.program_id(1) == 0)
    def _():
        xn_ref[...] = _rms(x_ref[...], nw_ref[...]).astype(BF16)

    dims = NT if w_rows else (((1,), (0,)), ((), ()))
    o_ref[...] = lax.dot_general(xn_ref[...], w_ref[...], dims,
                                 preferred_element_type=F32).astype(o_ref.dtype)


def norm_matmul(x, nw, w, layer, tm, tn, out_dtype, w_rows=False):
    m, k = x.shape
    n = w.shape[1] if w_rows else w.shape[2]
    w_spec = (pl.BlockSpec((None, tn, k), lambda i, j: (layer, j, 0)) if w_rows
              else pl.BlockSpec((None, k, tn), lambda i, j: (layer, 0, j)))
    return pl.pallas_call(
        functools.partial(_norm_matmul_kernel, w_rows=w_rows),
        grid=(m // tm, n // tn),
        in_specs=[pl.BlockSpec((tm, k), lambda i, j: (i, 0)),
                  pl.BlockSpec((1, k), lambda i, j: (0, 0)),
                  w_spec],
        out_specs=pl.BlockSpec((tm, tn), lambda i, j: (i, j)),
        out_shape=jax.ShapeDtypeStruct((m, n), out_dtype),
        scratch_shapes=[pltpu.VMEM((tm, k), BF16)],
        compiler_params=_params("parallel", "arbitrary"),
        name="norm_matmul",
    )(x, nw.reshape(1, k), w)


def _scan_constants(c, nlev, nmm):
    e = np.zeros((2, nmm, c, c), np.float32)
    for i in range(c):
        e[0, 0, i, :i + 1] = 1
        e[1, 0, i, i:] = 1
    for l in range(1, nmm):
        m = 1 << l
        for i in range(c):
            start = (i // m) * m
            end = start + m - 1
            if (i // m) % 2 == 1:
                e[0, l, i, start:i + 1] = 1
                e[1, l, i, start:i] = 1
            else:
                e[0, l, i, i + 1:end + 1] = 1
                e[1, l, i, i:end + 1] = 1
    t = np.arange(c)
    x = t[:, None] ^ t[None, :]
    lv = np.where(x > 0, np.floor(np.log2(np.maximum(x, 1))), nlev).astype(np.int32)
    lvf = np.where(t[:, None] >= t[None, :], lv, -1)
    masks = np.stack([[lvd == l for l in range(nlev + 1)] for lvd in (lvf, lvf.T)]).astype(np.float32)
    return e.reshape(2, nmm * c, c), masks


def _level_sums(b, l, rev):
    c = b.shape[0]
    m = 1 << l
    parts = []
    for j in range(c // (2 * m)):
        lo = 2 * m * j
        left, right = b[lo:lo + m], b[lo + m:lo + 2 * m]
        if rev:
            r = b[lo + m:lo + m + 1]
            parts += [left - r, r - right]
        else:
            r = b[lo + m - 1:lo + m]
            parts += [r - left, right - r]
    return jnp.concatenate(parts, axis=0)


def _scan_chunk(q, k, vs, la, st_refs, e, mk_ref, qmasks, rev):
    c, nlev, nmm = SCAN_CHUNK, SCAN_LEVELS, SCAN_MM_LEVELS
    la_hi = la.astype(BF16)
    la_lo = (la - la_hi.astype(F32)).astype(BF16)
    y = jnp.dot(e[0:c], jnp.concatenate([la_hi, la_lo], axis=1), preferred_element_type=F32)
    b = y[:, :LANES] + y[:, LANES:]
    x = jnp.dot(e[c:], la_hi, preferred_element_type=F32)
    odd = (lax.broadcasted_iota(jnp.int32, (c, 1), 0) & 1) == 1
    x0 = jnp.where(~odd if rev else odd, la, 0.0)
    tot = b[0:1] if rev else b[c - 1:c]
    qms = [(q * qm).astype(BF16) for qm in qmasks]
    kb = k.astype(BF16)
    acc = [lax.dot_general(qm, kb, NT, preferred_element_type=F32).astype(BF16) * mk_ref[nlev]
           for qm in qms]
    for l in range(nlev):
        if l == 0:
            xl = x0
        elif l < nmm:
            xl = x[(l - 1) * c:l * c]
        else:
            xl = _level_sums(b, l, rev)
        fl = jnp.exp(xl).astype(BF16)
        ksl = kb * fl
        al = lax.dot_general(jnp.concatenate([qm * fl for qm in qms], axis=0), ksl, NT,
                             preferred_element_type=F32)
        for h in range(len(qms)):
            acc[h] = acc[h] + al[h * c:(h + 1) * c].astype(BF16) * mk_ref[l]
    fq = jnp.exp(b).astype(BF16)
    ksk = kb * jnp.exp(tot - b).astype(BF16)
    decay = jnp.exp(tot)
    outs = []
    for h, qm in enumerate(qms):
        st = st_refs[h][...]
        o = jnp.dot(acc[h], vs[h], preferred_element_type=F32)
        o = o + lax.dot_general(qm * fq, st.astype(BF16), NT, preferred_element_type=F32)
        st_refs[h][...] = st * decay + lax.dot_general(vs[h], ksk, TN, preferred_element_type=F32)
        outs.append(o)
    return outs


def _scan_driver(prologue, qmasks, g_ref, nw_ref, e_ref, mk_ref, o_ref, of_ref, ob_ref, st_ref):
    c = SCAN_CHUNK
    s = of_ref.shape[0]
    nc = s // c
    nh = len(qmasks)
    st_ref[...] = jnp.zeros_like(st_ref)

    def one(r, d, dst_ref):
        q, k, v, la = prologue(r, d)
        vs = [v[:, h * LANES:(h + 1) * LANES] for h in range(nh)]
        outs = _scan_chunk(q, k, vs, la, [st_ref.at[d * nh + h] for h in range(nh)],
                           e_ref[d], mk_ref.at[d], qmasks, bool(d))
        for h in range(nh):
            dst_ref[pl.ds(r, c), h * LANES:(h + 1) * LANES] = outs[h]

    def body(n, carry):
        one(pl.multiple_of(n * c, c), 0, of_ref)
        one(pl.multiple_of((nc - 1 - n) * c, c), 1, ob_ref)
        return carry

    lax.fori_loop(0, nc, body, 0, unroll=2)

    def finish(n, carry):
        r = pl.multiple_of(n * c, c)
        for h in range(nh):
            cols = slice(h * LANES, (h + 1) * LANES)
            o = of_ref[pl.ds(r, c), cols] + ob_ref[pl.ds(r, c), cols]
            g = g_ref[0, pl.ds(r, c), cols]
            o_ref[0, pl.ds(r, c), cols] = (_rms(o, nw_ref[...]) * (g * _sigmoid(g))).astype(o_ref.dtype)
        return carry

    lax.fori_loop(0, nc, finish, 0)


def _gla_kernel(q_ref, k_ref, v_ref, g_ref, lr_ref, gw_ref, gb_ref, nw_ref, e_ref, mk_ref,
                o_ref, of_ref, ob_ref, st_ref):
    c = SCAN_CHUNK
    lane = lax.broadcasted_iota(jnp.int32, (1, LANES), 1)
    qmasks = [jnp.where(lane // GLA_DK == h, GLA_DK ** -0.5, 0.0).astype(F32)
              for h in range(LANES // GLA_DK)]

    def prologue(r, d):
        q = q_ref[0, pl.ds(r, c), :]
        k = k_ref[0, pl.ds(r, c), :]
        v = v_ref[0, pl.ds(r, c), :].astype(BF16)
        gk = jnp.dot(lr_ref[0, pl.ds(r, c), :].astype(BF16), gw_ref[d],
                     preferred_element_type=F32) + gb_ref[d]
        la = (jnp.minimum(gk, 0.0) - jnp.log(1.0 + jnp.exp(-jnp.abs(gk)))) * (1.0 / GLA_GATE_NORMALIZER)
        return q, k, v, la

    _scan_driver(prologue, qmasks, g_ref, nw_ref, e_ref, mk_ref, o_ref, of_ref, ob_ref, st_ref)


def _hgrn_kernel(q_ref, ff_ref, fb_ref, v_ref, g_ref, lb_ref, nw_ref, e_ref, mk_ref,
                 o_ref, of_ref, ob_ref, st_ref):
    c = SCAN_CHUNK
    qmasks = [jnp.full((1, LANES), HG_DK ** -0.5, F32)]

    def prologue(r, d):
        q = q_ref[0, pl.ds(r, c), :]
        q = q * _sigmoid(q)
        z = (fb_ref if d else ff_ref)[0, pl.ds(r, c), :]
        lb = lb_ref[d]
        f = jnp.clip(lb + (1.0 - lb) * _sigmoid(z), 1e-6, 1.0)
        v = v_ref[0, pl.ds(r, c), :].astype(BF16)
        return q, 1.0 - f, v, jnp.log(f)

    _scan_driver(prologue, qmasks, g_ref, nw_ref, e_ref, mk_ref, o_ref, of_ref, ob_ref, st_ref)


def _scan_call(kernel, name, proj, cols, extra, extra_specs, norm_w, steps, heads_per_step):
    b, s, _ = proj.shape
    e_np, mk_np = _scan_constants(SCAN_CHUNK, SCAN_LEVELS, SCAN_MM_LEVELS)
    e = jnp.asarray(e_np, BF16)
    mk = jnp.asarray(mk_np, BF16)
    ow = heads_per_step * LANES

    def col_spec(off, width):
        return pl.BlockSpec((1, s, width), lambda bi, h: (bi, 0, off // width + h))

    in_specs = [col_spec(off, width) for off, width in cols] + extra_specs + [
        pl.BlockSpec((1, LANES), lambda bi, h: (0, 0)),
        pl.BlockSpec(e.shape, lambda bi, h: (0, 0, 0)),
        pl.BlockSpec(mk.shape, lambda bi, h: (0, 0, 0, 0))]
    return pl.pallas_call(
        kernel,
        grid=(b, steps),
        in_specs=in_specs,
        out_specs=pl.BlockSpec((1, s, ow), lambda bi, h: (bi, 0, h)),
        out_shape=jax.ShapeDtypeStruct((b, s, steps * ow), BF16),
        scratch_shapes=[pltpu.VMEM((s, ow), F32), pltpu.VMEM((s, ow), F32),
                        pltpu.VMEM((2 * heads_per_step, LANES, LANES), F32)],
        compiler_params=_params("parallel", "parallel"),
        name=name,
    )(*([proj] * len(cols)), *extra, norm_w.reshape(1, LANES), e, mk)


def gla_group(proj, gk_w, gk_b, norm_w):
    gw = jnp.zeros((2, LANES, GLA_HEADS * GLA_DK), F32)
    gw = gw.at[0, 0:GLA_RANK].set(gk_w[0]).at[1, GLA_RANK:2 * GLA_RANK].set(gk_w[1]).astype(BF16)
    gb = gk_b.reshape(2, 1, GLA_HEADS * GLA_DK)
    pair = LANES // GLA_DK
    cols = [(COL_AQ, LANES), (COL_AK, LANES), (COL_AV, pair * GLA_DV), (COL_AG, pair * GLA_DV)]
    extra_specs = [pl.BlockSpec((1, proj.shape[1], LANES), lambda bi, h: (bi, 0, COL_LR // LANES)),
                   pl.BlockSpec((2, LANES, LANES), lambda bi, h: (0, 0, h)),
                   pl.BlockSpec((2, 1, LANES), lambda bi, h: (0, 0, h))]
    return _scan_call(_gla_kernel, "gla_scan", proj, cols, [proj, gw, gb], extra_specs, norm_w,
                      GLA_HEADS // pair, pair)


def hgrn_group(proj, lb, norm_w):
    cols = [(COL_BQ, LANES), (COL_BFF, LANES), (COL_BFB, LANES), (COL_BV, LANES), (COL_BG, LANES)]
    extra_specs = [pl.BlockSpec((2, 1, LANES), lambda bi, h: (0, 0, h))]
    return _scan_call(_hgrn_kernel, "hgrn_scan", proj, cols, [lb.reshape(2, 1, HG_HEADS * HG_DK)],
                      extra_specs, norm_w, HG_HEADS, 1)


def _rope(x, cos, sin_hi, sin_lo):
    return (x * cos + pltpu.roll(x, LANES - ROT_DIM // 2, 1) * sin_hi
            + pltpu.roll(x, ROT_DIM // 2, 1) * sin_lo)


def _swa_kernel(sink_ref, q_ref, k_ref, v_ref, t_ref, o_ref, kr_ref, vb_ref):
    kv = pl.program_id(1)
    n = pl.program_id(2)
    blk = SWA_BLOCK
    s = k_ref.shape[1]
    nb = s // blk
    nq = q_ref.shape[1] // blk
    rows = SWA_GROUP * blk

    @pl.when(n == 0)
    def _():
        zero = jnp.zeros((blk, SWA_HD), BF16)
        for ref in (kr_ref, vb_ref):
            ref[0:blk, :] = zero
            ref[blk + s:2 * blk + s, :] = zero
        for j in range(nb):
            r = slice(j * blk, (j + 1) * blk)
            kr_ref[blk + j * blk:2 * blk + j * blk, :] = _rope(
                k_ref[0, r, :], t_ref[0, r, :], t_ref[1, r, :], t_ref[2, r, :]).astype(BF16)
            vb_ref[blk + j * blk:2 * blk + j * blk, :] = v_ref[0, r, :].astype(BF16)

    qi = lax.broadcasted_iota(jnp.int32, (rows, 3 * blk), 0) & (blk - 1)
    ki = lax.broadcasted_iota(jnp.int32, (rows, 3 * blk), 1)
    in_window = jnp.abs(ki - blk - qi) <= SWA_WINDOW
    head = lax.broadcasted_iota(jnp.int32, (rows, 1), 0) // blk
    sink = jnp.zeros((rows, 1), F32)
    for g in range(SWA_GROUP):
        sink = jnp.where(head == g, sink_ref[0, kv * SWA_GROUP + g], sink)
    for j in range(nq):
        qb = n * nq + j
        r0 = pl.multiple_of(qb * blk, blk)
        cos, sin_hi, sin_lo = (t_ref[i, pl.ds(r0, blk), :] for i in range(3))
        q4 = jnp.concatenate(
            [_rope(q_ref[0, j * blk:(j + 1) * blk, g * SWA_HD:(g + 1) * SWA_HD], cos, sin_hi, sin_lo)
             for g in range(SWA_GROUP)], axis=0).astype(BF16)
        kb = kr_ref[pl.ds(r0, 3 * blk), :]
        vb = vb_ref[pl.ds(r0, 3 * blk), :]
        logits = lax.dot_general(q4, kb, NT, preferred_element_type=F32) * (SWA_HD ** -0.5)
        kblock = qb - 1 + ki // blk
        logits = jnp.where(in_window & (kblock >= 0) & (kblock < nb), logits, NEG_BIG)
        m = jnp.maximum(jnp.max(logits, axis=-1, keepdims=True), sink)
        p = jnp.exp(logits - m)
        denom = jnp.sum(p, axis=-1, keepdims=True) + jnp.exp(sink - m)
        o = jnp.dot((p / denom).astype(BF16), vb, preferred_element_type=F32)
        for g in range(SWA_GROUP):
            o_ref[0, j * blk:(j + 1) * blk, g * SWA_HD:(g + 1) * SWA_HD] = (
                o[g * blk:(g + 1) * blk].astype(o_ref.dtype))


def swa_group(proj, sink, rope_t):
    b, s, _ = proj.shape
    gw = SWA_GROUP * SWA_HD
    tq = SWA_Q_TILE

    def kv_spec(col):
        return pl.BlockSpec((1, s, SWA_HD), lambda bi, kv, n: (bi, 0, col // SWA_HD + kv))

    return pl.pallas_call(
        _swa_kernel,
        grid=(b, SWA_KV_HEADS, s // tq),
        in_specs=[pl.BlockSpec(memory_space=pltpu.SMEM),
                  pl.BlockSpec((1, tq, gw), lambda bi, kv, n: (bi, n, COL_CQ // gw + kv)),
                  kv_spec(COL_CK), kv_spec(COL_CV),
                  pl.BlockSpec(rope_t.shape, lambda bi, kv, n: (0, 0, 0))],
        out_specs=pl.BlockSpec((1, tq, gw), lambda bi, kv, n: (bi, n, kv)),
        out_shape=jax.ShapeDtypeStruct((b, s, SWA_HEADS * SWA_HD), BF16),
        scratch_shapes=[pltpu.VMEM((s + 2 * SWA_BLOCK, SWA_HD), BF16),
                        pltpu.VMEM((s + 2 * SWA_BLOCK, SWA_HD), BF16)],
        compiler_params=_params("parallel", "parallel", "arbitrary"),
        name="swa",
    )(sink.reshape(1, SWA_HEADS), proj, proj, proj, rope_t)


def rope_tables(s):
    half = ROT_DIM // 2
    inv = ROPE_THETA ** (-jnp.arange(0, ROT_DIM, 2, dtype=F32) / ROT_DIM)
    ang = jnp.arange(s, dtype=F32)[:, None] * inv[None, :]
    cos, sin = jnp.cos(ang), jnp.sin(ang)
    z = jnp.zeros((s, LANES - ROT_DIM), F32)
    zh = jnp.zeros((s, half), F32)
    return jnp.stack([jnp.concatenate([cos, cos, z + 1.0], axis=1),
                      jnp.concatenate([-sin, zh, z], axis=1),
                      jnp.concatenate([zh, sin, z], axis=1)])


def _post_mixer_kernel(oa_ref, ob_ref, oc_ref, x_ref, wout_ref, nq_ref, wq_ref, kv_ref, wo_ref,
                       nf_ref, rw_ref, xo_ref, xn_ref, aff_ref):
    ka, kb = oa_ref.shape[-1], ob_ref.shape[-1]
    x = x_ref[0]
    x = x + jnp.dot(oa_ref[0], wout_ref[0:ka, :], preferred_element_type=F32)
    x = x + jnp.dot(ob_ref[0], wout_ref[ka:ka + kb, :], preferred_element_type=F32)
    x = x + jnp.dot(oc_ref[0], wout_ref[ka + kb:, :], preferred_element_type=F32)
    q = jnp.dot(_rms(x, nq_ref[...]).astype(BF16), wq_ref[...], preferred_element_type=F32).astype(BF16)
    w = MEM_HEADS * MEM_HD
    outs = []
    for h in range(MEM_HEADS):
        k = kv_ref[0, :, h * MEM_HD:(h + 1) * MEM_HD]
        v = kv_ref[0, :, w + h * MEM_HD:w + (h + 1) * MEM_HD]
        logits = lax.dot_general(q[:, h * MEM_HD:(h + 1) * MEM_HD], k, NT,
                                 preferred_element_type=F32) * (MEM_HD ** -0.5)
        p = jnp.exp(logits - jnp.max(logits, axis=-1, keepdims=True))
        p = p / jnp.sum(p, axis=-1, keepdims=True)
        outs.append(jnp.dot(p.astype(BF16), v, preferred_element_type=F32).astype(BF16))
    x = x + jnp.dot(jnp.concatenate(outs, axis=-1), wo_ref[...], preferred_element_type=F32)
    xo_ref[0] = x
    xn = _rms(x, nf_ref[...]).astype(BF16)
    xn_ref[0] = xn
    logits = jnp.dot(xn, rw_ref[...], preferred_element_type=F32)
    lane = lax.broadcasted_iota(jnp.int32, logits.shape, 1)
    logits = jnp.where(lane < N_EXPERTS, logits, NEG_BIG)
    p = jnp.exp(logits - jnp.max(logits, axis=-1, keepdims=True))
    aff_ref[0] = p / jnp.sum(p, axis=-1, keepdims=True)


def post_mixer(oa, ob, oc, x, w_out, nq, wq, kv, wo, nf, rw, layer, tm):
    b, s, d = x.shape
    n_mem, kvw = kv.shape[1], kv.shape[2]

    def rows(width):
        return pl.BlockSpec((1, tm, width), lambda bi, i: (bi, i, 0))

    def whole(a):
        return pl.BlockSpec((None,) + a.shape[1:], lambda bi, i: (layer, 0, 0),
                            pipeline_mode=pl.Buffered(1))

    vec = pl.BlockSpec((1, d), lambda bi, i: (0, 0))
    return pl.pallas_call(
        _post_mixer_kernel,
        grid=(b, s // tm),
        in_specs=[rows(oa.shape[2]), rows(ob.shape[2]), rows(oc.shape[2]), rows(d), whole(w_out),
                  vec, whole(wq), pl.BlockSpec((1, n_mem, kvw), lambda bi, i: (bi, 0, 0)), whole(wo),
                  vec, whole(rw)],
        out_specs=[rows(d), rows(d), rows(LANES)],
        out_shape=[jax.ShapeDtypeStruct((b, s, d), F32), jax.ShapeDtypeStruct((b, s, d), BF16),
                   jax.ShapeDtypeStruct((b, s, LANES), F32)],
        compiler_params=_params("parallel", "parallel"),
        name="post_mixer",
    )(oa, ob, oc, x, w_out, nq.reshape(1, d), wq, kv, wo, nf.reshape(1, d), rw)


def _select_kernel(aff_ref, tri_ref, slot_ref, *, cap):
    aff = aff_ref[...]
    key = pltpu.bitcast(aff, jnp.int32)

    def bit_step(i, lo):
        cand = lo | (1 << (30 - i))
        cnt = jnp.sum(jnp.where(key >= cand, 1.0, 0.0), axis=-1, keepdims=True)
        return jnp.where(cnt >= cap, cand, lo)

    thr = lax.fori_loop(0, 31, bit_step, jnp.zeros((aff.shape[0], 1), jnp.int32))
    gt = key > thr
    eq = key == thr
    need = cap - jnp.sum(jnp.where(gt, 1.0, 0.0), axis=-1, keepdims=True)
    tri = tri_ref[...]
    eq_rank = jnp.dot(jnp.where(eq, 1.0, 0.0).astype(BF16), tri, preferred_element_type=F32)
    sel = gt | (eq & (eq_rank <= need))
    pos = jnp.dot(jnp.where(sel, 1.0, 0.0).astype(BF16), tri, preferred_element_type=F32)
    slot_ref[...] = jnp.where(sel, pos.astype(jnp.int32) - 1, -1)


def select(aff_t, cap):
    r, s = aff_t.shape
    tri = jnp.triu(jnp.ones((s, s), BF16))
    return pl.pallas_call(
        functools.partial(_select_kernel, cap=cap),
        grid=(1,),
        in_specs=[pl.BlockSpec((r, s), lambda i: (0, 0)), pl.BlockSpec((s, s), lambda i: (0, 0))],
        out_specs=pl.BlockSpec((r, s), lambda i: (0, 0)),
        out_shape=jax.ShapeDtypeStruct((r, s), jnp.int32),
        compiler_params=_params("arbitrary"),
        name="select",
    )(aff_t, tri)


def _gather_kernel(slot_ref, aff_ref, xn_ref, xe_ref, gate_ref, *, cap):
    slot = slot_ref[0]
    hit = lax.broadcasted_iota(jnp.int32, (cap, slot.shape[1]), 0) == slot
    xe_ref[0, 0] = jnp.dot(jnp.where(hit, 1.0, 0.0).astype(BF16), xn_ref[0],
                           preferred_element_type=F32).astype(BF16)
    gate_ref[0, 0] = jnp.sum(jnp.where(hit, aff_ref[0], 0.0), axis=-1, keepdims=True)


def gather(slot_t, aff_t, xn, cap):
    b, s, d = xn.shape
    ne = slot_t.shape[0] // b
    return pl.pallas_call(
        functools.partial(_gather_kernel, cap=cap),
        grid=(b, ne),
        in_specs=[pl.BlockSpec((1, 1, s), lambda bi, e: (bi * ne + e, 0, 0)),
                  pl.BlockSpec((1, 1, s), lambda bi, e: (bi * ne + e, 0, 0)),
                  pl.BlockSpec((1, s, d), lambda bi, e: (bi, 0, 0))],
        out_specs=[pl.BlockSpec((1, 1, cap, d), lambda bi, e: (bi, e, 0, 0)),
                   pl.BlockSpec((1, 1, cap, 1), lambda bi, e: (bi, e, 0, 0))],
        out_shape=[jax.ShapeDtypeStruct((b, ne, cap, d), BF16),
                   jax.ShapeDtypeStruct((b, ne, cap, 1), F32)],
        compiler_params=_params("parallel", "arbitrary"),
        name="moe_gather",
    )(slot_t.reshape(b * ne, 1, s), aff_t.reshape(b * ne, 1, s), xn)


def _ffn_kernel(xe_ref, gate_ref, wg_ref, wu_ref, wd_ref, y_ref, h_ref):
    st = pl.program_id(1)
    nf, rows, tf = h_ref.shape
    d = xe_ref.shape[-1]

    @pl.when(st < nf)
    def _():
        xe = xe_ref[...].reshape(rows, d)
        hg = jnp.dot(xe, wg_ref[0].astype(BF16), preferred_element_type=F32)
        hu = jnp.dot(xe, wu_ref[0].astype(BF16), preferred_element_type=F32)
        h_ref[st] = (hg * _sigmoid(hg) * hu * gate_ref[...].reshape(rows, 1)).astype(BF16)

    @pl.when(st >= nf)
    def _():
        y = jnp.dot(h_ref[0], wd_ref[0, 0:tf, :].astype(BF16), preferred_element_type=F32)
        for f in range(1, nf):
            y = y + jnp.dot(h_ref[f], wd_ref[0, f * tf:(f + 1) * tf, :].astype(BF16),
                            preferred_element_type=F32)
        y_ref[...] = y.reshape(y_ref.shape).astype(y_ref.dtype)


def expert_ffn(xe, gate, wg, wu, wd, layer, tf, tn):
    b, ne, cap, d = xe.shape
    dexp = wg.shape[3]
    nf = dexp // tf
    return pl.pallas_call(
        _ffn_kernel,
        grid=(ne, nf + d // tn),
        in_specs=[pl.BlockSpec((b, 1, cap, d), lambda e, st: (0, e, 0, 0)),
                  pl.BlockSpec((b, 1, cap, 1), lambda e, st: (0, e, 0, 0)),
                  pl.BlockSpec((None, 1, d, tf), lambda e, st: (layer, e, 0, jnp.minimum(st, nf - 1))),
                  pl.BlockSpec((None, 1, d, tf), lambda e, st: (layer, e, 0, jnp.minimum(st, nf - 1))),
                  pl.BlockSpec((None, 1, dexp, tn), lambda e, st: (layer, e, 0, jnp.maximum(st - nf, 0)))],
        out_specs=pl.BlockSpec((b, 1, cap, tn), lambda e, st: (0, e, 0, jnp.maximum(st - nf, 0))),
        out_shape=jax.ShapeDtypeStruct((b, ne, cap, d), BF16),
        scratch_shapes=[pltpu.VMEM((nf, b * cap, tf), BF16)],
        compiler_params=_params("parallel", "arbitrary"),
        name="expert_ffn",
    )(xe, gate, wg, wu, wd)


def _scatter_kernel(slot_ref, y_ref, x_ref, o_ref, *, cap):
    ne = y_ref.shape[1]
    s = x_ref.shape[1]
    lane = lax.broadcasted_iota(jnp.int32, (s, cap), 1)
    acc = x_ref[0]
    for e in range(ne):
        hit = slot_ref[0, :, e:e + 1] == lane
        acc = acc + jnp.dot(jnp.where(hit, 1.0, 0.0).astype(BF16), y_ref[0, e],
                            preferred_element_type=F32)
    o_ref[0] = acc


def scatter(slot, y, x, tn):
    b, s, d = x.shape
    ne, cap = y.shape[1], y.shape[2]
    return pl.pallas_call(
        functools.partial(_scatter_kernel, cap=cap),
        grid=(b, d // tn),
        in_specs=[pl.BlockSpec((1, s, ne), lambda bi, j: (bi, 0, 0)),
                  pl.BlockSpec((1, ne, cap, tn), lambda bi, j: (bi, 0, 0, j)),
                  pl.BlockSpec((1, s, tn), lambda bi, j: (bi, 0, j))],
        out_specs=pl.BlockSpec((1, s, tn), lambda bi, j: (bi, 0, j)),
        out_shape=jax.ShapeDtypeStruct((b, s, d), F32),
        compiler_params=_params("parallel", "parallel"),
        name="moe_scatter",
    )(slot, y, x)


def _norm_kernel(x_ref, nw_ref, o_ref):
    o_ref[...] = _rms(x_ref[...], nw_ref[...])


def final_norm(x, nw, tm):
    m, d = x.shape
    return pl.pallas_call(
        _norm_kernel,
        grid=(m // tm,),
        in_specs=[pl.BlockSpec((tm, d), lambda i: (i, 0)), pl.BlockSpec((1, d), lambda i: (0, 0))],
        out_specs=pl.BlockSpec((tm, d), lambda i: (i, 0)),
        out_shape=jax.ShapeDtypeStruct((m, d), F32),
        compiler_params=_params("parallel"),
        name="final_norm",
    )(x, nw.reshape(1, d))


def _reorder_kernel(main_ref, tail_ref, o_ref, *, n_head, n_body):
    j = pl.program_id(1)
    tr = o_ref.shape[0]
    lr = tail_ref.shape[0]

    @pl.when(j < n_head)
    def _():
        o_ref[...] = main_ref[...].astype(BF16)

    @pl.when((j >= n_head) & (j < n_head + n_body))
    def _():
        o_ref[0:tr - lr, :] = main_ref[lr:tr, :].astype(BF16)
        o_ref[tr - lr:tr, :] = tail_ref[...].astype(BF16)

    @pl.when(j == n_head + n_body)
    def _():
        o_ref[0:lr, :] = tail_ref[...].astype(BF16)
        o_ref[lr:tr, :] = jnp.zeros((tr - lr, o_ref.shape[1]), BF16)


def reorder_w_in(wt, tr):
    nl, rows, d = wt.shape
    lr = LR_END - LR_START
    assert rows == IN_COLS and LR_START % tr == 0 and (IN_COLS - LR_END) % tr == 0 and tr % lr == 0
    n_head, n_body = LR_START // tr, (IN_COLS - LR_END) // tr
    assert (n_head + n_body + 1) * tr == PROJ_COLS
    per = tr // lr
    return pl.pallas_call(
        functools.partial(_reorder_kernel, n_head=n_head, n_body=n_body),
        grid=(nl, PROJ_COLS // tr),
        in_specs=[pl.BlockSpec((None, tr, d), lambda l, j: (l, jnp.minimum(j, n_head + n_body - 1), 0)),
                  pl.BlockSpec((None, lr, d), lambda l, j: (
                      l, jnp.where(j < n_head + n_body, per * (j + 1), LR_START // lr), 0))],
        out_specs=pl.BlockSpec((None, tr, d), lambda l, j: (l, j, 0)),
        out_shape=jax.ShapeDtypeStruct((nl, PROJ_COLS, d), BF16),
        compiler_params=_params("parallel", "parallel"),
        name="reorder_w_in",
    )(wt, wt)


def kernel(x, mem, norm_mix, w_in, gla_gk_w, gla_gk_b, gla_norm, hgrn_lb_logits, hgrn_norm, swa_sink, w_out, norm_mem_q, norm_mem_kv, mem_wq, mem_wkv, mem_wo, norm_ffn, router_w, expert_w_gate, expert_w_up, expert_w_down, norm_final):
    b, s, d = x.shape
    depth = w_in.shape[0]
    n_mem = mem.shape[1]
    cap = EC_CAPACITY * s // N_EXPERTS
    rope_t = rope_tables(s)
    sm = jax.nn.softmax(hgrn_lb_logits.astype(F32), axis=0)
    lower_bounds = jnp.clip(jnp.cumsum(sm, axis=0) - sm[0], 0.0, 1.0)
    mem2 = mem.reshape(b * n_mem, d)
    w_in_r = reorder_w_in(jnp.swapaxes(w_in, 1, 2), 512)
    w_out_b, wq_b, wkv_b, wo_b = (w.astype(BF16) for w in (w_out, mem_wq, mem_wkv, mem_wo))
    rw_b = jnp.pad(router_w, ((0, 0), (0, 0), (0, LANES - N_EXPERTS))).astype(BF16)
    for l in range(depth):
        proj = norm_matmul(x.reshape(b * s, d), norm_mix[l], w_in_r, l, 1024, 1024, F32,
                           w_rows=True)
        proj = proj.reshape(b, s, PROJ_COLS)
        o_a = gla_group(proj, gla_gk_w[l], gla_gk_b[l], gla_norm[l])
        o_b = hgrn_group(proj, lower_bounds[l], hgrn_norm[l])
        o_c = swa_group(proj, swa_sink[l], rope_t)
        kv = norm_matmul(mem2, norm_mem_kv[l], wkv_b, l, b * n_mem, 512, BF16)
        x, xn, aff = post_mixer(o_a, o_b, o_c, x, w_out_b, norm_mem_q[l], wq_b,
                                kv.reshape(b, n_mem, -1), wo_b, norm_ffn[l], rw_b, l, 512)
        aff_t = jnp.swapaxes(aff[:, :, :N_EXPERTS], 1, 2).reshape(b * N_EXPERTS, s)
        slot_t = select(aff_t, cap)
        xe, gate = gather(slot_t, aff_t, xn, cap)
        y = expert_ffn(xe, gate, expert_w_gate, expert_w_up, expert_w_down, l, 512, 512)
        slot = jnp.swapaxes(slot_t.reshape(b, N_EXPERTS, s), 1, 2)
        x = scatter(slot, y, x, 512)
    return final_norm(x.reshape(b * s, d), norm_final, 1024).reshape(b, s, d)
```

```python
import functools

import numpy as np
import jax
import jax.numpy as jnp
from jax import lax
from jax.experimental import pallas as pl
from jax.experimental.pallas import tpu as pltpu

F32 = jnp.float32
BF16 = jnp.bfloat16

EPS = 1e-6
NEG_BIG = -1e30
LANES = 128
SUBLANES = 8

GLA_HEADS = 4
GLA_DK = 64
GLA_DV = 128
GLA_RANK = 16
GLA_GATE_NORMALIZER = 16.0
HG_HEADS = 4
HG_DK = 128
HG_DV = 128
SWA_HEADS = 8
SWA_KV_HEADS = 2
SWA_GROUP = SWA_HEADS // SWA_KV_HEADS
SWA_HD = 128
SWA_WINDOW = 128
SWA_BLOCK = 128
SWA_Q_TILE = 512
ROPE_THETA = 500000.0
ROT_DIM = SWA_HD // 4
MEM_HEADS = 4
MEM_HD = 128
N_EXPERTS = 16
EC_CAPACITY = 2

COL_AQ, COL_AK, COL_AV, COL_AG = 0, 256, 512, 1024
COL_BQ, COL_BFF, COL_BFB, COL_BV, COL_BG = 1536, 2048, 2560, 3072, 3584
COL_CQ, COL_CK, COL_CV = 4096, 5120, 5376
COL_LR = 5632
PROJ_COLS = 6144
LR_START, LR_END, IN_COLS = 1536, 1568, 5664

SCAN_CHUNK = 128
SCAN_LEVELS = 7
SCAN_MM_LEVELS = 3
assert (1 << SCAN_LEVELS) == SCAN_CHUNK and (1 << SCAN_MM_LEVELS) == SUBLANES
VMEM_LIMIT = 56 * 1024 * 1024

IN_PROJ_TM, IN_PROJ_TN = 1024, 1024
KV_PROJ_TN = 512
POST_MIXER_TM = 512
FFN_TF, FFN_TN = 512, 512
SCATTER_TN = 512
REORDER_ROWS = 512
FINAL_NORM_TM = 1024

NT = (((1,), (1,)), ((), ()))
TN = (((0,), (0,)), ((), ()))


def _params(*sem):
    return pltpu.CompilerParams(dimension_semantics=sem, vmem_limit_bytes=VMEM_LIMIT)


def _rms(x, w):
    return x * lax.rsqrt(jnp.mean(x * x, axis=-1, keepdims=True) + EPS) * w


def _sigmoid(x):
    return 0.5 * jnp.tanh(0.5 * x) + 0.5


def _norm_matmul_kernel(x_ref, nw_ref, w_ref, o_ref, xn_ref, *, w_rows):
    @pl.when(pl.program_id(1) == 0)
    def _():
        xn_ref[...] = _rms(x_ref[...], nw_ref[...]).astype(BF16)

    dims = NT if w_rows else (((1,), (0,)), ((), ()))
    o_ref[...] = lax.dot_general(xn_ref[...], w_ref[...], dims,
                                 preferred_element_type=F32).astype(o_ref.dtype)


def norm_matmul(x, nw, w, layer, tm, tn, out_dtype, w_rows=False):
    m, k = x.shape
    n = w.shape[1] if w_rows else w.shape[2]
    w_spec = (pl.BlockSpec((None, tn, k), lambda i, j: (layer, j, 0)) if w_rows
              else pl.BlockSpec((None, k, tn), lambda i, j: (layer, 0, j)))
    return pl.pallas_call(
        functools.partial(_norm_matmul_kernel, w_rows=w_rows),
        grid=(m // tm, n // tn),
        in_specs=[pl.BlockSpec((tm, k), lambda i, j: (i, 0)),
                  pl.BlockSpec((1, k), lambda i, j: (0, 0)),
                  w_spec],
        out_specs=pl.BlockSpec((tm, tn), lambda i, j: (i, j)),
        out_shape=jax.ShapeDtypeStruct((m, n), out_dtype),
        scratch_shapes=[pltpu.VMEM((tm, k), BF16)],
        compiler_params=_params("parallel", "arbitrary"),
        name="norm_matmul",
    )(x, nw.reshape(1, k), w)


def _scan_constants(c, nlev, nmm):
    e = np.zeros((2, nmm, c, c), np.float32)
    for i in range(c):
        e[0, 0, i, :i + 1] = 1
        e[1, 0, i, i:] = 1
    for l in range(1, nmm):
        m = 1 << l
        for i in range(c):
            start = (i // m) * m
            end = start + m - 1
            if (i // m) % 2 == 1:
                e[0, l, i, start:i + 1] = 1
                e[1, l, i, start:i] = 1
            else:
                e[0, l, i, i + 1:end + 1] = 1
                e[1, l, i, i:end + 1] = 1
    t = np.arange(c)
    x = t[:, None] ^ t[None, :]
    lv = np.where(x > 0, np.floor(np.log2(np.maximum(x, 1))), nlev).astype(np.int32)
    lvf = np.where(t[:, None] >= t[None, :], lv, -1)
    masks = np.stack([[lvd == l for l in range(nlev + 1)] for lvd in (lvf, lvf.T)]).astype(np.float32)
    return e.reshape(2, nmm * c, c), masks


def _level_sums(b, l, rev):
    c = b.shape[0]
    m = 1 << l
    parts = []
    for j in range(c // (2 * m)):
        lo = 2 * m * j
        left, right = b[lo:lo + m], b[lo + m:lo + 2 * m]
        if rev:
            r = b[lo + m:lo + m + 1]
            parts += [left - r, r - right]
        else:
            r = b[lo + m - 1:lo + m]
            parts += [r - left, right - r]
    return jnp.concatenate(parts, axis=0)


def _scan_chunk(q, k, vs, la, st_refs, e, mk_ref, qmasks, rev):
    c, nlev, nmm = SCAN_CHUNK, SCAN_LEVELS, SCAN_MM_LEVELS
    la_hi = la.astype(BF16)
    la_lo = (la - la_hi.astype(F32)).astype(BF16)
    y = jnp.dot(e[0:c], jnp.concatenate([la_hi, la_lo], axis=1), preferred_element_type=F32)
    b = y[:, :LANES] + y[:, LANES:]
    x = jnp.dot(e[c:], la_hi, preferred_element_type=F32)
    odd = (lax.broadcasted_iota(jnp.int32, (c, 1), 0) & 1) == 1
    x0 = jnp.where(~odd if rev else odd, la, 0.0)
    tot = b[0:1] if rev else b[c - 1:c]
    qms = [(q * qm).astype(BF16) for qm in qmasks]
    kb = k.astype(BF16)
    acc = [lax.dot_general(qm, kb, NT, preferred_element_type=F32).astype(BF16) * mk_ref[nlev]
           for qm in qms]
    for l in range(nlev):
        if l == 0:
            xl = x0
        elif l < nmm:
            xl = x[(l - 1) * c:l * c]
        else:
            xl = _level_sums(b, l, rev)
        fl = jnp.exp(xl).astype(BF16)
        ksl = kb * fl
        al = lax.dot_general(jnp.concatenate([qm * fl for qm in qms], axis=0), ksl, NT,
                             preferred_element_type=F32)
        for h in range(len(qms)):
            acc[h] = acc[h] + al[h * c:(h + 1) * c].astype(BF16) * mk_ref[l]
    fq = jnp.exp(b).astype(BF16)
    ksk = kb * jnp.exp(tot - b).astype(BF16)
    decay = jnp.exp(tot)
    outs = []
    for h, qm in enumerate(qms):
        st = st_refs[h][...]
        o = jnp.dot(acc[h], vs[h], preferred_element_type=F32)
        o = o + lax.dot_general(qm * fq, st.astype(BF16), NT, preferred_element_type=F32)
        st_refs[h][...] = st * decay + lax.dot_general(vs[h], ksk, TN, preferred_element_type=F32)
        outs.append(o)
    return outs


def _scan_driver(prologue, qmasks, g_ref, nw_ref, e_ref, mk_ref, o_ref, of_ref, ob_ref, st_ref):
    c = SCAN_CHUNK
    s = of_ref.shape[0]
    nc = s // c
    nh = len(qmasks)
    st_ref[...] = jnp.zeros_like(st_ref)

    def one(r, d, dst_ref):
        q, k, v, la = prologue(r, d)
        vs = [v[:, h * LANES:(h + 1) * LANES] for h in range(nh)]
        outs = _scan_chunk(q, k, vs, la, [st_ref.at[d * nh + h] for h in range(nh)],
                           e_ref[d], mk_ref.at[d], qmasks, bool(d))
        for h in range(nh):
            dst_ref[pl.ds(r, c), h * LANES:(h + 1) * LANES] = outs[h]

    def body(n, carry):
        one(pl.multiple_of(n * c, c), 0, of_ref)
        one(pl.multiple_of((nc - 1 - n) * c, c), 1, ob_ref)
        return carry

    lax.fori_loop(0, nc, body, 0, unroll=2)

    def finish(n, carry):
        r = pl.multiple_of(n * c, c)
        for h in range(nh):
            cols = slice(h * LANES, (h + 1) * LANES)
            o = of_ref[pl.ds(r, c), cols] + ob_ref[pl.ds(r, c), cols]
            g = g_ref[0, pl.ds(r, c), cols]
            o_ref[0, pl.ds(r, c), cols] = (_rms(o, nw_ref[...]) * (g * _sigmoid(g))).astype(o_ref.dtype)
        return carry

    lax.fori_loop(0, nc, finish, 0)


def _gla_kernel(q_ref, k_ref, v_ref, g_ref, lr_ref, gw_ref, gb_ref, nw_ref, e_ref, mk_ref,
                o_ref, of_ref, ob_ref, st_ref):
    c = SCAN_CHUNK
    lane = lax.broadcasted_iota(jnp.int32, (1, LANES), 1)
    qmasks = [jnp.where(lane // GLA_DK == h, GLA_DK ** -0.5, 0.0).astype(F32)
              for h in range(LANES // GLA_DK)]

    def prologue(r, d):
        q = q_ref[0, pl.ds(r, c), :]
        k = k_ref[0, pl.ds(r, c), :]
        v = v_ref[0, pl.ds(r, c), :].astype(BF16)
        gk = jnp.dot(lr_ref[0, pl.ds(r, c), :].astype(BF16), gw_ref[d],
                     preferred_element_type=F32) + gb_ref[d]
        la = (jnp.minimum(gk, 0.0) - jnp.log(1.0 + jnp.exp(-jnp.abs(gk)))) * (1.0 / GLA_GATE_NORMALIZER)
        return q, k, v, la

    _scan_driver(prologue, qmasks, g_ref, nw_ref, e_ref, mk_ref, o_ref, of_ref, ob_ref, st_ref)


def _hgrn_kernel(q_ref, ff_ref, fb_ref, v_ref, g_ref, lb_ref, nw_ref, e_ref, mk_ref,
                 o_ref, of_ref, ob_ref, st_ref):
    c = SCAN_CHUNK
    qmasks = [jnp.full((1, LANES), HG_DK ** -0.5, F32)]

    def prologue(r, d):
        q = q_ref[0, pl.ds(r, c), :]
        q = q * _sigmoid(q)
        z = (fb_ref if d else ff_ref)[0, pl.ds(r, c), :]
        lb = lb_ref[d]
        f = jnp.clip(lb + (1.0 - lb) * _sigmoid(z), 1e-6, 1.0)
        v = v_ref[0, pl.ds(r, c), :].astype(BF16)
        return q, 1.0 - f, v, jnp.log(f)

    _scan_driver(prologue, qmasks, g_ref, nw_ref, e_ref, mk_ref, o_ref, of_ref, ob_ref, st_ref)


def _scan_call(kernel, name, proj, cols, extra, extra_specs, norm_w, steps, heads_per_step):
    b, s, _ = proj.shape
    e_np, mk_np = _scan_constants(SCAN_CHUNK, SCAN_LEVELS, SCAN_MM_LEVELS)
    e = jnp.asarray(e_np, BF16)
    mk = jnp.asarray(mk_np, BF16)
    ow = heads_per_step * LANES

    def col_spec(off, width):
        return pl.BlockSpec((1, s, width), lambda bi, h: (bi, 0, off // width + h))

    in_specs = [col_spec(off, width) for off, width in cols] + extra_specs + [
        pl.BlockSpec((1, LANES), lambda bi, h: (0, 0)),
        pl.BlockSpec(e.shape, lambda bi, h: (0, 0, 0)),
        pl.BlockSpec(mk.shape, lambda bi, h: (0, 0, 0, 0))]
    return pl.pallas_call(
        kernel,
        grid=(b, steps),
        in_specs=in_specs,
        out_specs=pl.BlockSpec((1, s, ow), lambda bi, h: (bi, 0, h)),
        out_shape=jax.ShapeDtypeStruct((b, s, steps * ow), BF16),
        scratch_shapes=[pltpu.VMEM((s, ow), F32), pltpu.VMEM((s, ow), F32),
                        pltpu.VMEM((2 * heads_per_step, LANES, LANES), F32)],
        compiler_params=_params("parallel", "parallel"),
        name=name,
    )(*([proj] * len(cols)), *extra, norm_w.reshape(1, LANES), e, mk)


def gla_group(proj, gk_w, gk_b, norm_w):
    gw = jnp.zeros((2, LANES, GLA_HEADS * GLA_DK), F32)
    gw = gw.at[0, 0:GLA_RANK].set(gk_w[0]).at[1, GLA_RANK:2 * GLA_RANK].set(gk_w[1]).astype(BF16)
    gb = gk_b.reshape(2, 1, GLA_HEADS * GLA_DK)
    pair = LANES // GLA_DK
    cols = [(COL_AQ, LANES), (COL_AK, LANES), (COL_AV, pair * GLA_DV), (COL_AG, pair * GLA_DV)]
    extra_specs = [pl.BlockSpec((1, proj.shape[1], LANES), lambda bi, h: (bi, 0, COL_LR // LANES)),
                   pl.BlockSpec((2, LANES, LANES), lambda bi, h: (0, 0, h)),
                   pl.BlockSpec((2, 1, LANES), lambda bi, h: (0, 0, h))]
    return _scan_call(_gla_kernel, "gla_scan", proj, cols, [proj, gw, gb], extra_specs, norm_w,
                      GLA_HEADS // pair, pair)


def hgrn_group(proj, lb, norm_w):
    cols = [(COL_BQ, LANES), (COL_BFF, LANES), (COL_BFB, LANES), (COL_BV, LANES), (COL_BG, LANES)]
    extra_specs = [pl.BlockSpec((2, 1, LANES), lambda bi, h: (0, 0, h))]
    return _scan_call(_hgrn_kernel, "hgrn_scan", proj, cols, [lb.reshape(2, 1, HG_HEADS * HG_DK)],
                      extra_specs, norm_w, HG_HEADS, 1)


def _rope(x, cos, sin_hi, sin_lo):
    return (x * cos + pltpu.roll(x, LANES - ROT_DIM // 2, 1) * sin_hi
            + pltpu.roll(x, ROT_DIM // 2, 1) * sin_lo)


def _swa_kernel(sink_ref, q_ref, k_ref, v_ref, t_ref, o_ref, kr_ref, vb_ref):
    kv = pl.program_id(1)
    n = pl.program_id(2)
    blk = SWA_BLOCK
    s = k_ref.shape[1]
    nb = s // blk
    nq = q_ref.shape[1] // blk
    rows = SWA_GROUP * blk

    @pl.when(n == 0)
    def _():
        zero = jnp.zeros((blk, SWA_HD), BF16)
        for ref in (kr_ref, vb_ref):
            ref[0:blk, :] = zero
            ref[blk + s:2 * blk + s, :] = zero
        for j in range(nb):
            r = slice(j * blk, (j + 1) * blk)
            kr_ref[blk + j * blk:2 * blk + j * blk, :] = _rope(
                k_ref[0, r, :], t_ref[0, r, :], t_ref[1, r, :], t_ref[2, r, :]).astype(BF16)
            vb_ref[blk + j * blk:2 * blk + j * blk, :] = v_ref[0, r, :].astype(BF16)

    qi = lax.broadcasted_iota(jnp.int32, (rows, 3 * blk), 0) & (blk - 1)
    ki = lax.broadcasted_iota(jnp.int32, (rows, 3 * blk), 1)
    window_bias = jnp.where(jnp.abs(ki - blk - qi) <= SWA_WINDOW, 0.0, NEG_BIG)
    col_block = lax.broadcasted_iota(jnp.int32, (1, 3 * blk), 1) // blk
    head = lax.broadcasted_iota(jnp.int32, (rows, 1), 0) // blk
    sink = jnp.zeros((rows, 1), F32)
    for g in range(SWA_GROUP):
        sink = jnp.where(head == g, sink_ref[0, kv * SWA_GROUP + g], sink)
    for j in range(nq):
        qb = n * nq + j
        r0 = pl.multiple_of(qb * blk, blk)
        cos, sin_hi, sin_lo = (t_ref[i, pl.ds(r0, blk), :] for i in range(3))
        q4 = jnp.concatenate(
            [_rope(q_ref[0, j * blk:(j + 1) * blk, g * SWA_HD:(g + 1) * SWA_HD], cos, sin_hi, sin_lo)
             for g in range(SWA_GROUP)], axis=0).astype(BF16)
        kb = kr_ref[pl.ds(r0, 3 * blk), :]
        vb = vb_ref[pl.ds(r0, 3 * blk), :]
        logits = lax.dot_general(q4, kb, NT, preferred_element_type=F32) * (SWA_HD ** -0.5)
        off_seq = ((col_block == 0) & (qb == 0)) | ((col_block == 2) & (qb == nb - 1))
        logits = logits + window_bias + jnp.where(off_seq, NEG_BIG, 0.0)
        m = jnp.maximum(jnp.max(logits, axis=-1, keepdims=True), sink)
        p = jnp.exp(logits - m)
        denom = jnp.sum(p, axis=-1, keepdims=True) + jnp.exp(sink - m)
        o = jnp.dot((p / denom).astype(BF16), vb, preferred_element_type=F32)
        for g in range(SWA_GROUP):
            o_ref[0, j * blk:(j + 1) * blk, g * SWA_HD:(g + 1) * SWA_HD] = (
                o[g * blk:(g + 1) * blk].astype(o_ref.dtype))


def swa_group(proj, sink, rope_t):
    b, s, _ = proj.shape
    gw = SWA_GROUP * SWA_HD
    tq = SWA_Q_TILE

    def kv_spec(col):
        return pl.BlockSpec((1, s, SWA_HD), lambda bi, kv, n: (bi, 0, col // SWA_HD + kv))

    return pl.pallas_call(
        _swa_kernel,
        grid=(b, SWA_KV_HEADS, s // tq),
        in_specs=[pl.BlockSpec(memory_space=pltpu.SMEM),
                  pl.BlockSpec((1, tq, gw), lambda bi, kv, n: (bi, n, COL_CQ // gw + kv)),
                  kv_spec(COL_CK), kv_spec(COL_CV),
                  pl.BlockSpec(rope_t.shape, lambda bi, kv, n: (0, 0, 0))],
        out_specs=pl.BlockSpec((1, tq, gw), lambda bi, kv, n: (bi, n, kv)),
        out_shape=jax.ShapeDtypeStruct((b, s, SWA_HEADS * SWA_HD), BF16),
        scratch_shapes=[pltpu.VMEM((s + 2 * SWA_BLOCK, SWA_HD), BF16),
                        pltpu.VMEM((s + 2 * SWA_BLOCK, SWA_HD), BF16)],
        compiler_params=_params("parallel", "parallel", "arbitrary"),
        name="swa",
    )(sink.reshape(1, SWA_HEADS), proj, proj, proj, rope_t)


def rope_tables(s):
    half = ROT_DIM // 2
    inv = ROPE_THETA ** (-jnp.arange(0, ROT_DIM, 2, dtype=F32) / ROT_DIM)
    ang = jnp.arange(s, dtype=F32)[:, None] * inv[None, :]
    cos, sin = jnp.cos(ang), jnp.sin(ang)
    z = jnp.zeros((s, LANES - ROT_DIM), F32)
    zh = jnp.zeros((s, half), F32)
    return jnp.stack([jnp.concatenate([cos, cos, z + 1.0], axis=1),
                      jnp.concatenate([-sin, zh, z], axis=1),
                      jnp.concatenate([zh, sin, z], axis=1)])


def _post_mixer_kernel(oa_ref, ob_ref, oc_ref, x_ref, wout_ref, nq_ref, wq_ref, kv_ref, wo_ref,
                       nf_ref, rw_ref, xo_ref, xn_ref, aff_ref):
    ka, kb = oa_ref.shape[-1], ob_ref.shape[-1]
    x = x_ref[0]
    x = x + jnp.dot(oa_ref[0], wout_ref[0:ka, :], preferred_element_type=F32)
    x = x + jnp.dot(ob_ref[0], wout_ref[ka:ka + kb, :], preferred_element_type=F32)
    x = x + jnp.dot(oc_ref[0], wout_ref[ka + kb:, :], preferred_element_type=F32)
    q = jnp.dot(_rms(x, nq_ref[...]).astype(BF16), wq_ref[...], preferred_element_type=F32).astype(BF16)
    w = MEM_HEADS * MEM_HD
    outs = []
    for h in range(MEM_HEADS):
        k = kv_ref[0, :, h * MEM_HD:(h + 1) * MEM_HD]
        v = kv_ref[0, :, w + h * MEM_HD:w + (h + 1) * MEM_HD]
        logits = lax.dot_general(q[:, h * MEM_HD:(h + 1) * MEM_HD], k, NT,
                                 preferred_element_type=F32) * (MEM_HD ** -0.5)
        p = jnp.exp(logits - jnp.max(logits, axis=-1, keepdims=True))
        o = jnp.dot(p.astype(BF16), v, preferred_element_type=F32) / jnp.sum(p, axis=-1, keepdims=True)
        outs.append(o.astype(BF16))
    x = x + jnp.dot(jnp.concatenate(outs, axis=-1), wo_ref[...], preferred_element_type=F32)
    xo_ref[0] = x
    xn = _rms(x, nf_ref[...]).astype(BF16)
    xn_ref[0] = xn
    logits = jnp.dot(xn, rw_ref[...], preferred_element_type=F32)
    lane = lax.broadcasted_iota(jnp.int32, logits.shape, 1)
    logits = jnp.where(lane < N_EXPERTS, logits, NEG_BIG)
    p = jnp.exp(logits - jnp.max(logits, axis=-1, keepdims=True))
    aff_ref[0] = p / jnp.sum(p, axis=-1, keepdims=True)


def post_mixer(oa, ob, oc, x, w_out, nq, wq, kv, wo, nf, rw, layer, tm):
    b, s, d = x.shape
    n_mem, kvw = kv.shape[1], kv.shape[2]

    def rows(width):
        return pl.BlockSpec((1, tm, width), lambda bi, i: (bi, i, 0))

    def whole(a):
        return pl.BlockSpec((None,) + a.shape[1:], lambda bi, i: (layer, 0, 0),
                            pipeline_mode=pl.Buffered(1))

    vec = pl.BlockSpec((1, d), lambda bi, i: (0, 0))
    return pl.pallas_call(
        _post_mixer_kernel,
        grid=(b, s // tm),
        in_specs=[rows(oa.shape[2]), rows(ob.shape[2]), rows(oc.shape[2]), rows(d), whole(w_out),
                  vec, whole(wq), pl.BlockSpec((1, n_mem, kvw), lambda bi, i: (bi, 0, 0)), whole(wo),
                  vec, whole(rw)],
        out_specs=[rows(d), rows(d), rows(LANES)],
        out_shape=[jax.ShapeDtypeStruct((b, s, d), F32), jax.ShapeDtypeStruct((b, s, d), BF16),
                   jax.ShapeDtypeStruct((b, s, LANES), F32)],
        compiler_params=_params("parallel", "parallel"),
        name="post_mixer",
    )(oa, ob, oc, x, w_out, nq.reshape(1, d), wq, kv, wo, nf.reshape(1, d), rw)


def _select_kernel(aff_ref, tri_ref, slot_ref, *, cap):
    aff = aff_ref[...]
    key = pltpu.bitcast(aff, jnp.int32)

    def bit_step(i, lo):
        cand = lo | (1 << (30 - i))
        cnt = jnp.sum(jnp.where(key >= cand, 1.0, 0.0), axis=-1, keepdims=True)
        return jnp.where(cnt >= cap, cand, lo)

    thr = lax.fori_loop(0, 31, bit_step, jnp.zeros((aff.shape[0], 1), jnp.int32))
    gt = key > thr
    eq = key == thr
    need = cap - jnp.sum(jnp.where(gt, 1.0, 0.0), axis=-1, keepdims=True)
    tri = tri_ref[...]
    eq_rank = jnp.dot(jnp.where(eq, 1.0, 0.0).astype(BF16), tri, preferred_element_type=F32)
    sel = gt | (eq & (eq_rank <= need))
    pos = jnp.dot(jnp.where(sel, 1.0, 0.0).astype(BF16), tri, preferred_element_type=F32)
    slot_ref[...] = jnp.where(sel, pos.astype(jnp.int32) - 1, -1)


def select(aff_t, cap):
    r, s = aff_t.shape
    tri = jnp.triu(jnp.ones((s, s), BF16))
    return pl.pallas_call(
        functools.partial(_select_kernel, cap=cap),
        grid=(1,),
        in_specs=[pl.BlockSpec((r, s), lambda i: (0, 0)), pl.BlockSpec((s, s), lambda i: (0, 0))],
        out_specs=pl.BlockSpec((r, s), lambda i: (0, 0)),
        out_shape=jax.ShapeDtypeStruct((r, s), jnp.int32),
        compiler_params=_params("arbitrary"),
        name="select",
    )(aff_t, tri)


def _gather_kernel(slot_ref, aff_ref, xn_ref, xe_ref, gate_ref, *, cap):
    slot = slot_ref[0]
    hit = lax.broadcasted_iota(jnp.int32, (cap, slot.shape[1]), 0) == slot
    xe_ref[0, 0] = jnp.dot(jnp.where(hit, 1.0, 0.0).astype(BF16), xn_ref[0],
                           preferred_element_type=F32).astype(BF16)
    gate_ref[0, 0] = jnp.sum(jnp.where(hit, aff_ref[0], 0.0), axis=-1, keepdims=True)


def gather(slot_t, aff_t, xn, cap):
    b, s, d = xn.shape
    ne = slot_t.shape[0] // b
    return pl.pallas_call(
        functools.partial(_gather_kernel, cap=cap),
        grid=(b, ne),
        in_specs=[pl.BlockSpec((1, 1, s), lambda bi, e: (bi * ne + e, 0, 0)),
                  pl.BlockSpec((1, 1, s), lambda bi, e: (bi * ne + e, 0, 0)),
                  pl.BlockSpec((1, s, d), lambda bi, e: (bi, 0, 0))],
        out_specs=[pl.BlockSpec((1, 1, cap, d), lambda bi, e: (bi, e, 0, 0)),
                   pl.BlockSpec((1, 1, cap, 1), lambda bi, e: (bi, e, 0, 0))],
        out_shape=[jax.ShapeDtypeStruct((b, ne, cap, d), BF16),
                   jax.ShapeDtypeStruct((b, ne, cap, 1), F32)],
        compiler_params=_params("parallel", "arbitrary"),
        name="moe_gather",
    )(slot_t.reshape(b * ne, 1, s), aff_t.reshape(b * ne, 1, s), xn)


def _ffn_kernel(xe_ref, gate_ref, wg_ref, wu_ref, wd_ref, y_ref, h_ref):
    st = pl.program_id(1)
    nf, rows, tf = h_ref.shape
    d = xe_ref.shape[-1]

    @pl.when(st < nf)
    def _():
        xe = xe_ref[...].reshape(rows, d)
        hg = jnp.dot(xe, wg_ref[0].astype(BF16), preferred_element_type=F32)
        hu = jnp.dot(xe, wu_ref[0].astype(BF16), preferred_element_type=F32)
        h_ref[st] = (hg * _sigmoid(hg) * hu * gate_ref[...].reshape(rows, 1)).astype(BF16)

    @pl.when(st >= nf)
    def _():
        y = jnp.dot(h_ref[0], wd_ref[0, 0:tf, :].astype(BF16), preferred_element_type=F32)
        for f in range(1, nf):
            y = y + jnp.dot(h_ref[f], wd_ref[0, f * tf:(f + 1) * tf, :].astype(BF16),
                            preferred_element_type=F32)
        y_ref[...] = y.reshape(y_ref.shape).astype(y_ref.dtype)


def expert_ffn(xe, gate, wg, wu, wd, layer, tf, tn):
    b, ne, cap, d = xe.shape
    dexp = wg.shape[3]
    nf = dexp // tf
    return pl.pallas_call(
        _ffn_kernel,
        grid=(ne, nf + d // tn),
        in_specs=[pl.BlockSpec((b, 1, cap, d), lambda e, st: (0, e, 0, 0)),
                  pl.BlockSpec((b, 1, cap, 1), lambda e, st: (0, e, 0, 0)),
                  pl.BlockSpec((None, 1, d, tf), lambda e, st: (layer, e, 0, jnp.minimum(st, nf - 1))),
                  pl.BlockSpec((None, 1, d, tf), lambda e, st: (layer, e, 0, jnp.minimum(st, nf - 1))),
                  pl.BlockSpec((None, 1, dexp, tn), lambda e, st: (layer, e, 0, jnp.maximum(st - nf, 0)))],
        out_specs=pl.BlockSpec((b, 1, cap, tn), lambda e, st: (0, e, 0, jnp.maximum(st - nf, 0))),
        out_shape=jax.ShapeDtypeStruct((b, ne, cap, d), BF16),
        scratch_shapes=[pltpu.VMEM((nf, b * cap, tf), BF16)],
        compiler_params=_params("parallel", "arbitrary"),
        name="expert_ffn",
    )(xe, gate, wg, wu, wd)


def _scatter_kernel(slot_ref, y_ref, x_ref, o_ref, *, cap):
    ne = y_ref.shape[1]
    s = x_ref.shape[1]
    lane = lax.broadcasted_iota(jnp.int32, (s, cap), 1)
    acc = x_ref[0]
    for e in range(ne):
        hit = slot_ref[0, :, e:e + 1] == lane
        acc = acc + jnp.dot(jnp.where(hit, 1.0, 0.0).astype(BF16), y_ref[0, e],
                            preferred_element_type=F32)
    o_ref[0] = acc


def scatter(slot, y, x, tn):
    b, s, d = x.shape
    ne, cap = y.shape[1], y.shape[2]
    return pl.pallas_call(
        functools.partial(_scatter_kernel, cap=cap),
        grid=(b, d // tn),
        in_specs=[pl.BlockSpec((1, s, ne), lambda bi, j: (bi, 0, 0)),
                  pl.BlockSpec((1, ne, cap, tn), lambda bi, j: (bi, 0, 0, j)),
                  pl.BlockSpec((1, s, tn), lambda bi, j: (bi, 0, j))],
        out_specs=pl.BlockSpec((1, s, tn), lambda bi, j: (bi, 0, j)),
        out_shape=jax.ShapeDtypeStruct((b, s, d), F32),
        compiler_params=_params("parallel", "parallel"),
        name="moe_scatter",
    )(slot, y, x)


def _norm_kernel(x_ref, nw_ref, o_ref):
    o_ref[...] = _rms(x_ref[...], nw_ref[...])


def final_norm(x, nw, tm):
    m, d = x.shape
    return pl.pallas_call(
        _norm_kernel,
        grid=(m // tm,),
        in_specs=[pl.BlockSpec((tm, d), lambda i: (i, 0)), pl.BlockSpec((1, d), lambda i: (0, 0))],
        out_specs=pl.BlockSpec((tm, d), lambda i: (i, 0)),
        out_shape=jax.ShapeDtypeStruct((m, d), F32),
        compiler_params=_params("parallel"),
        name="final_norm",
    )(x, nw.reshape(1, d))


def _reorder_kernel(main_ref, tail_ref, o_ref, *, n_head, n_body):
    j = pl.program_id(1)
    tr = o_ref.shape[0]
    lr = tail_ref.shape[0]

    @pl.when(j < n_head)
    def _():
        o_ref[...] = main_ref[...].astype(BF16)

    @pl.when((j >= n_head) & (j < n_head + n_body))
    def _():
        o_ref[0:tr - lr, :] = main_ref[lr:tr, :].astype(BF16)
        o_ref[tr - lr:tr, :] = tail_ref[...].astype(BF16)

    @pl.when(j == n_head + n_body)
    def _():
        o_ref[0:lr, :] = tail_ref[...].astype(BF16)
        o_ref[lr:tr, :] = jnp.zeros((tr - lr, o_ref.shape[1]), BF16)


def reorder_w_in(wt, tr):
    nl, rows, d = wt.shape
    lr = LR_END - LR_START
    assert rows == IN_COLS and LR_START % tr == 0 and (IN_COLS - LR_END) % tr == 0 and tr % lr == 0
    n_head, n_body = LR_START // tr, (IN_COLS - LR_END) // tr
    assert (n_head + n_body + 1) * tr == PROJ_COLS
    per = tr // lr
    return pl.pallas_call(
        functools.partial(_reorder_kernel, n_head=n_head, n_body=n_body),
        grid=(nl, PROJ_COLS // tr),
        in_specs=[pl.BlockSpec((None, tr, d), lambda l, j: (l, jnp.minimum(j, n_head + n_body - 1), 0)),
                  pl.BlockSpec((None, lr, d), lambda l, j: (
                      l, jnp.where(j < n_head + n_body, per * (j + 1), LR_START // lr), 0))],
        out_specs=pl.BlockSpec((None, tr, d), lambda l, j: (l, j, 0)),
        out_shape=jax.ShapeDtypeStruct((nl, PROJ_COLS, d), BF16),
        compiler_params=_params("parallel", "parallel"),
        name="reorder_w_in",
    )(wt, wt)


def kernel(x, mem, norm_mix, w_in, gla_gk_w, gla_gk_b, gla_norm, hgrn_lb_logits, hgrn_norm, swa_sink, w_out, norm_mem_q, norm_mem_kv, mem_wq, mem_wkv, mem_wo, norm_ffn, router_w, expert_w_gate, expert_w_up, expert_w_down, norm_final):
    b, s, d = x.shape
    depth = w_in.shape[0]
    n_mem = mem.shape[1]
    cap = EC_CAPACITY * s // N_EXPERTS
    rope_t = rope_tables(s)
    sm = jax.nn.softmax(hgrn_lb_logits.astype(F32), axis=0)
    lower_bounds = jnp.clip(jnp.cumsum(sm, axis=0) - sm[0], 0.0, 1.0)
    mem2 = mem.reshape(b * n_mem, d)
    w_in_r = reorder_w_in(jnp.swapaxes(w_in, 1, 2), REORDER_ROWS)
    w_out_b, wq_b, wkv_b, wo_b = (w.astype(BF16) for w in (w_out, mem_wq, mem_wkv, mem_wo))
    rw_b = jnp.pad(router_w, ((0, 0), (0, 0), (0, LANES - N_EXPERTS))).astype(BF16)
    for l in range(depth):
        proj = norm_matmul(x.reshape(b * s, d), norm_mix[l], w_in_r, l, IN_PROJ_TM, IN_PROJ_TN, F32,
                           w_rows=True)
        proj = proj.reshape(b, s, PROJ_COLS)
        o_a = gla_group(proj, gla_gk_w[l], gla_gk_b[l], gla_norm[l])
        o_b = hgrn_group(proj, lower_bounds[l], hgrn_norm[l])
        o_c = swa_group(proj, swa_sink[l], rope_t)
        kv = norm_matmul(mem2, norm_mem_kv[l], wkv_b, l, b * n_mem, KV_PROJ_TN, BF16)
        x, xn, aff = post_mixer(o_a, o_b, o_c, x, w_out_b, norm_mem_q[l], wq_b,
                                kv.reshape(b, n_mem, -1), wo_b, norm_ffn[l], rw_b, l, POST_MIXER_TM)
        aff_t = jnp.swapaxes(aff[:, :, :N_EXPERTS], 1, 2).reshape(b * N_EXPERTS, s)
        slot_t = select(aff_t, cap)
        xe, gate = gather(slot_t, aff_t, xn, cap)
        y = expert_ffn(xe, gate, expert_w_gate, expert_w_up, expert_w_down, l, FFN_TF, FFN_TN)
        slot = jnp.swapaxes(slot_t.reshape(b, N_EXPERTS, s), 1, 2)
        x = scatter(slot, y, x, SCATTER_TN)
    return final_norm(x.reshape(b * s, d), norm_final, FINAL_NORM_TM).reshape(b, s, d)
```

```python
import functools

import numpy as np
import jax
import jax.numpy as jnp
from jax import lax
from jax.experimental import pallas as pl
from jax.experimental.pallas import tpu as pltpu

F32 = jnp.float32
BF16 = jnp.bfloat16

EPS = 1e-6
NEG_BIG = -1e30
LANES = 128
SUBLANES = 8

GLA_HEADS = 4
GLA_DK = 64
GLA_DV = 128
GLA_RANK = 16
GLA_GATE_NORMALIZER = 16.0
HG_HEADS = 4
HG_DK = 128
HG_DV = 128
SWA_HEADS = 8
SWA_KV_HEADS = 2
SWA_GROUP = SWA_HEADS // SWA_KV_HEADS
SWA_HD = 128
SWA_WINDOW = 128
SWA_BLOCK = 128
SWA_Q_TILE = 512
ROPE_THETA = 500000.0
ROT_DIM = SWA_HD // 4
MEM_HEADS = 4
MEM_HD = 128
N_EXPERTS = 16
EC_CAPACITY = 2

COL_AQ, COL_AK, COL_AV, COL_AG = 0, 256, 512, 1024
COL_BQ, COL_BFF, COL_BFB, COL_BV, COL_BG = 1536, 2048, 2560, 3072, 3584
COL_CQ, COL_CK, COL_CV = 4096, 5120, 5376
COL_LR = 5632
PROJ_COLS = 6144
LR_START, LR_END, IN_COLS = 1536, 1568, 5664

SCAN_CHUNK = 128
SCAN_LEVELS = 7
SCAN_MM_LEVELS = 3
assert (1 << SCAN_LEVELS) == SCAN_CHUNK and (1 << SCAN_MM_LEVELS) == SUBLANES
VMEM_LIMIT = 56 * 1024 * 1024

IN_PROJ_TM, IN_PROJ_TN = 1024, 1024
KV_PROJ_TN = 512
POST_MIXER_TM = 512
FFN_TF, FFN_TN = 512, 512
SCATTER_TN = 512
REORDER_ROWS = 512
FINAL_NORM_TM = 1024

NT = (((1,), (1,)), ((), ()))
TN = (((0,), (0,)), ((), ()))


def _params(*sem):
    return pltpu.CompilerParams(dimension_semantics=sem, vmem_limit_bytes=VMEM_LIMIT)


def _rms(x, w):
    return x * lax.rsqrt(jnp.mean(x * x, axis=-1, keepdims=True) + EPS) * w


def _sigmoid(x):
    return 0.5 * jnp.tanh(0.5 * x) + 0.5


def _norm_matmul_kernel(x_ref, nw_ref, w_ref, o_ref, xn_ref, *, w_rows):
    @pl.when(pl.program_id(1) == 0)
    def _():
        xn_ref[...] = _rms(x_ref[...], nw_ref[...]).astype(BF16)

    dims = NT if w_rows else (((1,), (0,)), ((), ()))
    o_ref[...] = lax.dot_general(xn_ref[...], w_ref[...], dims,
                                 preferred_element_type=F32).astype(o_ref.dtype)


def norm_matmul(x, nw, w, layer, tm, tn, out_dtype, w_rows=False):
    m, k = x.shape
    n = w.shape[1] if w_rows else w.shape[2]
    w_spec = (pl.BlockSpec((None, tn, k), lambda i, j: (layer, j, 0)) if w_rows
              else pl.BlockSpec((None, k, tn), lambda i, j: (layer, 0, j)))
    return pl.pallas_call(
        functools.partial(_norm_matmul_kernel, w_rows=w_rows),
        grid=(m // tm, n // tn),
        in_specs=[pl.BlockSpec((tm, k), lambda i, j: (i, 0)),
                  pl.BlockSpec((1, k), lambda i, j: (0, 0)),
                  w_spec],
        out_specs=pl.BlockSpec((tm, tn), lambda i, j: (i, j)),
        out_shape=jax.ShapeDtypeStruct((m, n), out_dtype),
        scratch_shapes=[pltpu.VMEM((tm, k), BF16)],
        compiler_params=_params("parallel", "arbitrary"),
        name="norm_matmul",
    )(x, nw.reshape(1, k), w)


def norm_matmul_layers(x, nw, w, tn, out_dtype):
    m, k = x.shape
    nl, _, n = w.shape
    return pl.pallas_call(
        functools.partial(_norm_matmul_kernel, w_rows=False),
        grid=(nl, n // tn),
        in_specs=[pl.BlockSpec((m, k), lambda l, j: (0, 0)),
                  pl.BlockSpec((None, 1, k), lambda l, j: (l, 0, 0)),
                  pl.BlockSpec((None, k, tn), lambda l, j: (l, 0, j))],
        out_specs=pl.BlockSpec((None, m, tn), lambda l, j: (l, 0, j)),
        out_shape=jax.ShapeDtypeStruct((nl, m, n), out_dtype),
        scratch_shapes=[pltpu.VMEM((m, k), BF16)],
        compiler_params=_params("parallel", "arbitrary"),
        name="norm_matmul_layers",
    )(x, nw.reshape(nl, 1, k), w)


def _scan_constants(c, nlev, nmm):
    e = np.zeros((2, nmm, c, c), np.float32)
    for i in range(c):
        e[0, 0, i, :i + 1] = 1
        e[1, 0, i, i:] = 1
    for l in range(1, nmm):
        m = 1 << l
        for i in range(c):
            start = (i // m) * m
            end = start + m - 1
            if (i // m) % 2 == 1:
                e[0, l, i, start:i + 1] = 1
                e[1, l, i, start:i] = 1
            else:
                e[0, l, i, i + 1:end + 1] = 1
                e[1, l, i, i:end + 1] = 1
    t = np.arange(c)
    x = t[:, None] ^ t[None, :]
    lv = np.where(x > 0, np.floor(np.log2(np.maximum(x, 1))), nlev).astype(np.int32)
    lvf = np.where(t[:, None] >= t[None, :], lv, -1)
    masks = np.stack([[lvd == l for l in range(nlev + 1)] for lvd in (lvf, lvf.T)]).astype(np.float32)
    return e.reshape(2, nmm * c, c), masks


def _level_sums(b, l, rev):
    c = b.shape[0]
    m = 1 << l
    parts = []
    for j in range(c // (2 * m)):
        lo = 2 * m * j
        left, right = b[lo:lo + m], b[lo + m:lo + 2 * m]
        if rev:
            r = b[lo + m:lo + m + 1]
            parts += [left - r, r - right]
        else:
            r = b[lo + m - 1:lo + m]
            parts += [r - left, right - r]
    return jnp.concatenate(parts, axis=0)


def _running_sums(la, rev):
    c = la.shape[0]
    row = lax.broadcasted_iota(jnp.int32, (c, 1), 0)
    b = la
    sh = 1
    while sh < c:
        if rev:
            b = b + jnp.where(row < c - sh, pltpu.roll(b, c - sh, 0), 0.0)
        else:
            b = b + jnp.where(row >= sh, pltpu.roll(b, sh, 0), 0.0)
        sh *= 2
    return b


def _scan_chunk(q, k, vs, la, st_refs, e, mk_ref, qmasks, rev, vpu_sums):
    c, nlev, nmm = SCAN_CHUNK, SCAN_LEVELS, SCAN_MM_LEVELS
    la_hi = la.astype(BF16)
    if vpu_sums:
        b = _running_sums(la, rev)
    else:
        la_lo = (la - la_hi.astype(F32)).astype(BF16)
        y = jnp.dot(e[0:c], jnp.concatenate([la_hi, la_lo], axis=1), preferred_element_type=F32)
        b = y[:, :LANES] + y[:, LANES:]
    x = jnp.dot(e[c:], la_hi, preferred_element_type=F32)
    odd = (lax.broadcasted_iota(jnp.int32, (c, 1), 0) & 1) == 1
    x0 = jnp.where(~odd if rev else odd, la, 0.0)
    tot = b[0:1] if rev else b[c - 1:c]
    qms = [(q * qm).astype(BF16) for qm in qmasks]
    kb = k.astype(BF16)
    acc = [lax.dot_general(qm, kb, NT, preferred_element_type=F32).astype(BF16) * mk_ref[nlev]
           for qm in qms]
    for l in range(nlev):
        if l == 0:
            xl = x0
        elif l < nmm:
            xl = x[(l - 1) * c:l * c]
        else:
            xl = _level_sums(b, l, rev)
        fl = jnp.exp(xl).astype(BF16)
        ksl = kb * fl
        al = lax.dot_general(jnp.concatenate([qm * fl for qm in qms], axis=0), ksl, NT,
                             preferred_element_type=F32)
        for h in range(len(qms)):
            acc[h] = acc[h] + al[h * c:(h + 1) * c].astype(BF16) * mk_ref[l]
    fq = jnp.exp(b).astype(BF16)
    ksk = kb * jnp.exp(tot - b).astype(BF16)
    decay = jnp.exp(tot)
    outs = []
    for h, qm in enumerate(qms):
        st = st_refs[h][...]
        o = jnp.dot(acc[h], vs[h], preferred_element_type=F32)
        o = o + lax.dot_general(qm * fq, st.astype(BF16), NT, preferred_element_type=F32)
        st_refs[h][...] = st * decay + lax.dot_general(vs[h], ksk, TN, preferred_element_type=F32)
        outs.append(o)
    return outs


def _scan_driver(prologue, qmasks, vpu_sums, g_ref, nw_ref, e_ref, mk_ref, o_ref, of_ref, ob_ref, st_ref):
    c = SCAN_CHUNK
    s = of_ref.shape[0]
    nc = s // c
    nh = len(qmasks)
    st_ref[...] = jnp.zeros_like(st_ref)

    def one(r, d, dst_ref):
        q, k, v, la = prologue(r, d)
        vs = [v[:, h * LANES:(h + 1) * LANES] for h in range(nh)]
        outs = _scan_chunk(q, k, vs, la, [st_ref.at[d * nh + h] for h in range(nh)],
                           e_ref[d], mk_ref.at[d], qmasks, bool(d), vpu_sums)
        for h in range(nh):
            dst_ref[pl.ds(r, c), h * LANES:(h + 1) * LANES] = outs[h]

    def body(n, carry):
        one(pl.multiple_of(n * c, c), 0, of_ref)
        one(pl.multiple_of((nc - 1 - n) * c, c), 1, ob_ref)
        return carry

    lax.fori_loop(0, nc, body, 0, unroll=2)

    def finish(n, carry):
        r = pl.multiple_of(n * c, c)
        for h in range(nh):
            cols = slice(h * LANES, (h + 1) * LANES)
            o = of_ref[pl.ds(r, c), cols] + ob_ref[pl.ds(r, c), cols]
            g = g_ref[0, pl.ds(r, c), cols]
            o_ref[0, pl.ds(r, c), cols] = (_rms(o, nw_ref[...]) * (g * _sigmoid(g))).astype(o_ref.dtype)
        return carry

    lax.fori_loop(0, nc, finish, 0)


def _gla_kernel(q_ref, k_ref, v_ref, g_ref, lr_ref, gw_ref, gb_ref, nw_ref, e_ref, mk_ref,
                o_ref, of_ref, ob_ref, st_ref, la_ref):
    c = SCAN_CHUNK
    lane = lax.broadcasted_iota(jnp.int32, (1, LANES), 1)
    qmasks = [jnp.where(lane // GLA_DK == h, GLA_DK ** -0.5, 0.0).astype(F32)
              for h in range(LANES // GLA_DK)]
    lr = lr_ref[0].astype(BF16)
    for d in range(2):
        gk = jnp.dot(lr, gw_ref[d], preferred_element_type=F32) + gb_ref[d]
        la_ref[d] = (jnp.minimum(gk, 0.0) - jnp.log(1.0 + jnp.exp(-jnp.abs(gk)))) * (
            1.0 / GLA_GATE_NORMALIZER)

    def prologue(r, d):
        q = q_ref[0, pl.ds(r, c), :]
        k = k_ref[0, pl.ds(r, c), :]
        v = v_ref[0, pl.ds(r, c), :].astype(BF16)
        return q, k, v, la_ref[d, pl.ds(r, c), :]

    _scan_driver(prologue, qmasks, False, g_ref, nw_ref, e_ref, mk_ref, o_ref, of_ref, ob_ref, st_ref)


def _hgrn_kernel(q_ref, ff_ref, fb_ref, v_ref, g_ref, lb_ref, nw_ref, e_ref, mk_ref,
                 o_ref, of_ref, ob_ref, st_ref):
    c = SCAN_CHUNK
    qmasks = [jnp.full((1, LANES), HG_DK ** -0.5, F32)]

    def prologue(r, d):
        q = q_ref[0, pl.ds(r, c), :]
        q = q * _sigmoid(q)
        z = (fb_ref if d else ff_ref)[0, pl.ds(r, c), :]
        lb = lb_ref[d]
        f = jnp.clip(lb + (1.0 - lb) * _sigmoid(z), 1e-6, 1.0)
        v = v_ref[0, pl.ds(r, c), :].astype(BF16)
        return q, 1.0 - f, v, jnp.log(f)

    _scan_driver(prologue, qmasks, True, g_ref, nw_ref, e_ref, mk_ref, o_ref, of_ref, ob_ref, st_ref)


def _scan_call(kernel, name, proj, cols, extra, extra_specs, norm_w, steps, heads_per_step,
               more_scratch=()):
    b, s, _ = proj.shape
    e_np, mk_np = _scan_constants(SCAN_CHUNK, SCAN_LEVELS, SCAN_MM_LEVELS)
    e = jnp.asarray(e_np, BF16)
    mk = jnp.asarray(mk_np, BF16)
    ow = heads_per_step * LANES

    def col_spec(off, width):
        return pl.BlockSpec((1, s, width), lambda bi, h: (bi, 0, off // width + h))

    in_specs = [col_spec(off, width) for off, width in cols] + extra_specs + [
        pl.BlockSpec((1, LANES), lambda bi, h: (0, 0)),
        pl.BlockSpec(e.shape, lambda bi, h: (0, 0, 0)),
        pl.BlockSpec(mk.shape, lambda bi, h: (0, 0, 0, 0))]
    return pl.pallas_call(
        kernel,
        grid=(b, steps),
        in_specs=in_specs,
        out_specs=pl.BlockSpec((1, s, ow), lambda bi, h: (bi, 0, h)),
        out_shape=jax.ShapeDtypeStruct((b, s, steps * ow), BF16),
        scratch_shapes=[pltpu.VMEM((s, ow), F32), pltpu.VMEM((s, ow), F32),
                        pltpu.VMEM((2 * heads_per_step, LANES, LANES), F32)] + list(more_scratch),
        compiler_params=_params("parallel", "parallel"),
        name=name,
    )(*([proj] * len(cols)), *extra, norm_w.reshape(1, LANES), e, mk)


def gla_group(proj, gk_w, gk_b, norm_w):
    gw = jnp.zeros((2, LANES, GLA_HEADS * GLA_DK), F32)
    gw = gw.at[0, 0:GLA_RANK].set(gk_w[0]).at[1, GLA_RANK:2 * GLA_RANK].set(gk_w[1]).astype(BF16)
    gb = gk_b.reshape(2, 1, GLA_HEADS * GLA_DK)
    pair = LANES // GLA_DK
    cols = [(COL_AQ, LANES), (COL_AK, LANES), (COL_AV, pair * GLA_DV), (COL_AG, pair * GLA_DV)]
    extra_specs = [pl.BlockSpec((1, proj.shape[1], LANES), lambda bi, h: (bi, 0, COL_LR // LANES)),
                   pl.BlockSpec((2, LANES, LANES), lambda bi, h: (0, 0, h)),
                   pl.BlockSpec((2, 1, LANES), lambda bi, h: (0, 0, h))]
    return _scan_call(_gla_kernel, "gla_scan", proj, cols, [proj, gw, gb], extra_specs, norm_w,
                      GLA_HEADS // pair, pair,
                      more_scratch=[pltpu.VMEM((2, proj.shape[1], LANES), F32)])


def hgrn_group(proj, lb, norm_w):
    cols = [(COL_BQ, LANES), (COL_BFF, LANES), (COL_BFB, LANES), (COL_BV, LANES), (COL_BG, LANES)]
    extra_specs = [pl.BlockSpec((2, 1, LANES), lambda bi, h: (0, 0, h))]
    return _scan_call(_hgrn_kernel, "hgrn_scan", proj, cols, [lb.reshape(2, 1, HG_HEADS * HG_DK)],
                      extra_specs, norm_w, HG_HEADS, 1)


def _rope(x, cos, sin_hi, sin_lo):
    return (x * cos + pltpu.roll(x, LANES - ROT_DIM // 2, 1) * sin_hi
            + pltpu.roll(x, ROT_DIM // 2, 1) * sin_lo)


def _swa_kernel(sink_ref, q_ref, k_ref, v_ref, t_ref, o_ref, kr_ref, vb_ref):
    kv = pl.program_id(1)
    n = pl.program_id(2)
    blk = SWA_BLOCK
    s = k_ref.shape[1]
    nb = s // blk
    nq = q_ref.shape[1] // blk
    rows = SWA_GROUP * blk

    @pl.when(n == 0)
    def _():
        zero = jnp.zeros((blk, SWA_HD), BF16)
        for ref in (kr_ref, vb_ref):
            ref[0:blk, :] = zero
            ref[blk + s:2 * blk + s, :] = zero
        for j in range(nb):
            r = slice(j * blk, (j + 1) * blk)
            kr_ref[blk + j * blk:2 * blk + j * blk, :] = _rope(
                k_ref[0, r, :], t_ref[0, r, :], t_ref[1, r, :], t_ref[2, r, :]).astype(BF16)
            vb_ref[blk + j * blk:2 * blk + j * blk, :] = v_ref[0, r, :].astype(BF16)

    qi = lax.broadcasted_iota(jnp.int32, (rows, 3 * blk), 0) & (blk - 1)
    ki = lax.broadcasted_iota(jnp.int32, (rows, 3 * blk), 1)
    window_bias = jnp.where(jnp.abs(ki - blk - qi) <= SWA_WINDOW, 0.0, NEG_BIG)
    col_block = lax.broadcasted_iota(jnp.int32, (1, 3 * blk), 1) // blk
    head = lax.broadcasted_iota(jnp.int32, (rows, 1), 0) // blk
    sink = jnp.zeros((rows, 1), F32)
    for g in range(SWA_GROUP):
        sink = jnp.where(head == g, sink_ref[0, kv * SWA_GROUP + g], sink)
    for j in range(nq):
        qb = n * nq + j
        r0 = pl.multiple_of(qb * blk, blk)
        cos, sin_hi, sin_lo = (t_ref[i, pl.ds(r0, blk), :] * (SWA_HD ** -0.5) for i in range(3))
        q4 = jnp.concatenate(
            [_rope(q_ref[0, j * blk:(j + 1) * blk, g * SWA_HD:(g + 1) * SWA_HD], cos, sin_hi, sin_lo)
             for g in range(SWA_GROUP)], axis=0).astype(BF16)
        kb = kr_ref[pl.ds(r0, 3 * blk), :]
        vb = vb_ref[pl.ds(r0, 3 * blk), :]
        logits = lax.dot_general(q4, kb, NT, preferred_element_type=F32)
        off_seq = ((col_block == 0) & (qb == 0)) | ((col_block == 2) & (qb == nb - 1))
        logits = logits + window_bias + jnp.where(off_seq, NEG_BIG, 0.0)
        m = jnp.maximum(jnp.max(logits, axis=-1, keepdims=True), sink)
        p = jnp.exp(logits - m)
        denom = jnp.sum(p, axis=-1, keepdims=True) + jnp.exp(sink - m)
        o = jnp.dot((p / denom).astype(BF16), vb, preferred_element_type=F32)
        for g in range(SWA_GROUP):
            o_ref[0, j * blk:(j + 1) * blk, g * SWA_HD:(g + 1) * SWA_HD] = (
                o[g * blk:(g + 1) * blk].astype(o_ref.dtype))


def swa_group(proj, sink, rope_t):
    b, s, _ = proj.shape
    gw = SWA_GROUP * SWA_HD
    tq = SWA_Q_TILE

    def kv_spec(col):
        return pl.BlockSpec((1, s, SWA_HD), lambda bi, kv, n: (bi, 0, col // SWA_HD + kv))

    return pl.pallas_call(
        _swa_kernel,
        grid=(b, SWA_KV_HEADS, s // tq),
        in_specs=[pl.BlockSpec(memory_space=pltpu.SMEM),
                  pl.BlockSpec((1, tq, gw), lambda bi, kv, n: (bi, n, COL_CQ // gw + kv)),
                  kv_spec(COL_CK), kv_spec(COL_CV),
                  pl.BlockSpec(rope_t.shape, lambda bi, kv, n: (0, 0, 0))],
        out_specs=pl.BlockSpec((1, tq, gw), lambda bi, kv, n: (bi, n, kv)),
        out_shape=jax.ShapeDtypeStruct((b, s, SWA_HEADS * SWA_HD), BF16),
        scratch_shapes=[pltpu.VMEM((s + 2 * SWA_BLOCK, SWA_HD), BF16),
                        pltpu.VMEM((s + 2 * SWA_BLOCK, SWA_HD), BF16)],
        compiler_params=_params("parallel", "parallel", "arbitrary"),
        name="swa",
    )(sink.reshape(1, SWA_HEADS), proj, proj, proj, rope_t)


def rope_tables(s):
    half = ROT_DIM // 2
    inv = ROPE_THETA ** (-jnp.arange(0, ROT_DIM, 2, dtype=F32) / ROT_DIM)
    ang = jnp.arange(s, dtype=F32)[:, None] * inv[None, :]
    cos, sin = jnp.cos(ang), jnp.sin(ang)
    z = jnp.zeros((s, LANES - ROT_DIM), F32)
    zh = jnp.zeros((s, half), F32)
    return jnp.stack([jnp.concatenate([cos, cos, z + 1.0], axis=1),
                      jnp.concatenate([-sin, zh, z], axis=1),
                      jnp.concatenate([zh, sin, z], axis=1)])


def _post_mixer_kernel(oa_ref, ob_ref, oc_ref, x_ref, wout_ref, nq_ref, wq_ref, kv_ref, wo_ref,
                       nf_ref, rw_ref, xo_ref, xn_ref, aff_ref):
    ka, kb = oa_ref.shape[-1], ob_ref.shape[-1]
    x = x_ref[0]
    x = x + jnp.dot(oa_ref[0], wout_ref[0:ka, :], preferred_element_type=F32)
    x = x + jnp.dot(ob_ref[0], wout_ref[ka:ka + kb, :], preferred_element_type=F32)
    x = x + jnp.dot(oc_ref[0], wout_ref[ka + kb:, :], preferred_element_type=F32)
    q = jnp.dot(_rms(x, nq_ref[...]).astype(BF16), wq_ref[...], preferred_element_type=F32).astype(BF16)
    w = MEM_HEADS * MEM_HD
    outs = []
    for h in range(MEM_HEADS):
        k = kv_ref[0, :, h * MEM_HD:(h + 1) * MEM_HD]
        v = kv_ref[0, :, w + h * MEM_HD:w + (h + 1) * MEM_HD]
        logits = lax.dot_general(q[:, h * MEM_HD:(h + 1) * MEM_HD], k, NT,
                                 preferred_element_type=F32) * (MEM_HD ** -0.5)
        p = jnp.exp(logits - jnp.max(logits, axis=-1, keepdims=True))
        o = jnp.dot(p.astype(BF16), v, preferred_element_type=F32) / jnp.sum(p, axis=-1, keepdims=True)
        outs.append(o.astype(BF16))
    x = x + jnp.dot(jnp.concatenate(outs, axis=-1), wo_ref[...], preferred_element_type=F32)
    xo_ref[0] = x
    xn = _rms(x, nf_ref[...]).astype(BF16)
    xn_ref[0] = xn
    logits = jnp.dot(xn, rw_ref[...], preferred_element_type=F32)
    lane = lax.broadcasted_iota(jnp.int32, logits.shape, 1)
    logits = jnp.where(lane < N_EXPERTS, logits, NEG_BIG)
    p = jnp.exp(logits - jnp.max(logits, axis=-1, keepdims=True))
    aff_ref[0] = p / jnp.sum(p, axis=-1, keepdims=True)


def post_mixer(oa, ob, oc, x, w_out, nq, wq, kv, wo, nf, rw, layer, tm):
    b, s, d = x.shape
    n_mem, kvw = kv.shape[2], kv.shape[3]

    def rows(width):
        return pl.BlockSpec((1, tm, width), lambda bi, i: (bi, i, 0))

    def whole(a):
        return pl.BlockSpec((None,) + a.shape[1:], lambda bi, i: (layer, 0, 0),
                            pipeline_mode=pl.Buffered(1))

    vec = pl.BlockSpec((1, d), lambda bi, i: (0, 0))
    return pl.pallas_call(
        _post_mixer_kernel,
        grid=(b, s // tm),
        in_specs=[rows(oa.shape[2]), rows(ob.shape[2]), rows(oc.shape[2]), rows(d), whole(w_out),
                  vec, whole(wq),
                  pl.BlockSpec((None, 1, n_mem, kvw), lambda bi, i: (layer, bi, 0, 0)), whole(wo),
                  vec, whole(rw)],
        out_specs=[rows(d), rows(d), rows(LANES)],
        out_shape=[jax.ShapeDtypeStruct((b, s, d), F32), jax.ShapeDtypeStruct((b, s, d), BF16),
                   jax.ShapeDtypeStruct((b, s, LANES), F32)],
        compiler_params=_params("parallel", "parallel"),
        name="post_mixer",
    )(oa, ob, oc, x, w_out, nq.reshape(1, d), wq, kv, wo, nf.reshape(1, d), rw)


def _select_kernel(aff_ref, tri_ref, slot_ref, *, cap):
    aff = aff_ref[...]
    key = pltpu.bitcast(aff, jnp.int32)

    def bit_step(i, lo):
        cand = lo | (1 << (30 - i))
        cnt = jnp.sum(jnp.where(key >= cand, 1.0, 0.0), axis=-1, keepdims=True)
        return jnp.where(cnt >= cap, cand, lo)

    thr = lax.fori_loop(0, 31, bit_step, jnp.zeros((aff.shape[0], 1), jnp.int32))
    gt = key > thr
    eq = key == thr
    need = cap - jnp.sum(jnp.where(gt, 1.0, 0.0), axis=-1, keepdims=True)
    tri = tri_ref[...]
    eq_rank = jnp.dot(jnp.where(eq, 1.0, 0.0).astype(BF16), tri, preferred_element_type=F32)
    sel = gt | (eq & (eq_rank <= need))
    pos = jnp.dot(jnp.where(sel, 1.0, 0.0).astype(BF16), tri, preferred_element_type=F32)
    slot_ref[...] = jnp.where(sel, pos.astype(jnp.int32) - 1, -1)


def select(aff_t, cap):
    r, s = aff_t.shape
    tri = jnp.triu(jnp.ones((s, s), BF16))
    return pl.pallas_call(
        functools.partial(_select_kernel, cap=cap),
        grid=(1,),
        in_specs=[pl.BlockSpec((r, s), lambda i: (0, 0)), pl.BlockSpec((s, s), lambda i: (0, 0))],
        out_specs=pl.BlockSpec((r, s), lambda i: (0, 0)),
        out_shape=jax.ShapeDtypeStruct((r, s), jnp.int32),
        compiler_params=_params("arbitrary"),
        name="select",
    )(aff_t, tri)


def _gather_kernel(slot_ref, aff_ref, xn_ref, xe_ref, gate_ref, *, cap):
    slot = slot_ref[0]
    hit = lax.broadcasted_iota(jnp.int32, (cap, slot.shape[1]), 0) == slot
    xe_ref[0, 0] = jnp.dot(jnp.where(hit, 1.0, 0.0).astype(BF16), xn_ref[0],
                           preferred_element_type=F32).astype(BF16)
    gate_ref[0, 0] = jnp.sum(jnp.where(hit, aff_ref[0], 0.0), axis=-1, keepdims=True)


def gather(slot_t, aff_t, xn, cap):
    b, s, d = xn.shape
    ne = slot_t.shape[0] // b
    return pl.pallas_call(
        functools.partial(_gather_kernel, cap=cap),
        grid=(b, ne),
        in_specs=[pl.BlockSpec((1, 1, s), lambda bi, e: (bi * ne + e, 0, 0)),
                  pl.BlockSpec((1, 1, s), lambda bi, e: (bi * ne + e, 0, 0)),
                  pl.BlockSpec((1, s, d), lambda bi, e: (bi, 0, 0))],
        out_specs=[pl.BlockSpec((1, 1, cap, d), lambda bi, e: (bi, e, 0, 0)),
                   pl.BlockSpec((1, 1, cap, 1), lambda bi, e: (bi, e, 0, 0))],
        out_shape=[jax.ShapeDtypeStruct((b, ne, cap, d), BF16),
                   jax.ShapeDtypeStruct((b, ne, cap, 1), F32)],
        compiler_params=_params("parallel", "arbitrary"),
        name="moe_gather",
    )(slot_t.reshape(b * ne, 1, s), aff_t.reshape(b * ne, 1, s), xn)


def _ffn_kernel(xe_ref, gate_ref, wg_ref, wu_ref, wd_ref, y_ref, h_ref):
    st = pl.program_id(1)
    nf, rows, tf = h_ref.shape
    d = xe_ref.shape[-1]

    @pl.when(st < nf)
    def _():
        xe = xe_ref[...].reshape(rows, d)
        hg = jnp.dot(xe, wg_ref[0].astype(BF16), preferred_element_type=F32)
        hu = jnp.dot(xe, wu_ref[0].astype(BF16), preferred_element_type=F32)
        h_ref[st] = (hg * _sigmoid(hg) * hu * gate_ref[...].reshape(rows, 1)).astype(BF16)

    @pl.when(st >= nf)
    def _():
        y = jnp.dot(h_ref[0], wd_ref[0, 0:tf, :].astype(BF16), preferred_element_type=F32)
        for f in range(1, nf):
            y = y + jnp.dot(h_ref[f], wd_ref[0, f * tf:(f + 1) * tf, :].astype(BF16),
                            preferred_element_type=F32)
        y_ref[...] = y.reshape(y_ref.shape).astype(y_ref.dtype)


def expert_ffn(xe, gate, wg, wu, wd, layer, tf, tn):
    b, ne, cap, d = xe.shape
    dexp = wg.shape[3]
    nf = dexp // tf
    return pl.pallas_call(
        _ffn_kernel,
        grid=(ne, nf + d // tn),
        in_specs=[pl.BlockSpec((b, 1, cap, d), lambda e, st: (0, e, 0, 0)),
                  pl.BlockSpec((b, 1, cap, 1), lambda e, st: (0, e, 0, 0)),
                  pl.BlockSpec((None, 1, d, tf), lambda e, st: (layer, e, 0, jnp.minimum(st, nf - 1))),
                  pl.BlockSpec((None, 1, d, tf), lambda e, st: (layer, e, 0, jnp.minimum(st, nf - 1))),
                  pl.BlockSpec((None, 1, dexp, tn), lambda e, st: (layer, e, 0, jnp.maximum(st - nf, 0)))],
        out_specs=pl.BlockSpec((b, 1, cap, tn), lambda e, st: (0, e, 0, jnp.maximum(st - nf, 0))),
        out_shape=jax.ShapeDtypeStruct((b, ne, cap, d), BF16),
        scratch_shapes=[pltpu.VMEM((nf, b * cap, tf), BF16)],
        compiler_params=_params("parallel", "arbitrary"),
        name="expert_ffn",
    )(xe, gate, wg, wu, wd)


def _scatter_kernel(slot_ref, y_ref, x_ref, o_ref, *, cap):
    ne = y_ref.shape[1]
    s = x_ref.shape[1]
    lane = lax.broadcasted_iota(jnp.int32, (s, cap), 1)
    acc = x_ref[0]
    for e in range(ne):
        hit = slot_ref[0, :, e:e + 1] == lane
        acc = acc + jnp.dot(jnp.where(hit, 1.0, 0.0).astype(BF16), y_ref[0, e],
                            preferred_element_type=F32)
    o_ref[0] = acc


def scatter(slot, y, x, tn):
    b, s, d = x.shape
    ne, cap = y.shape[1], y.shape[2]
    return pl.pallas_call(
        functools.partial(_scatter_kernel, cap=cap),
        grid=(b, d // tn),
        in_specs=[pl.BlockSpec((1, s, ne), lambda bi, j: (bi, 0, 0)),
                  pl.BlockSpec((1, ne, cap, tn), lambda bi, j: (bi, 0, 0, j)),
                  pl.BlockSpec((1, s, tn), lambda bi, j: (bi, 0, j))],
        out_specs=pl.BlockSpec((1, s, tn), lambda bi, j: (bi, 0, j)),
        out_shape=jax.ShapeDtypeStruct((b, s, d), F32),
        compiler_params=_params("parallel", "parallel"),
        name="moe_scatter",
    )(slot, y, x)


def _norm_kernel(x_ref, nw_ref, o_ref):
    o_ref[...] = _rms(x_ref[...], nw_ref[...])


def final_norm(x, nw, tm):
    m, d = x.shape
    return pl.pallas_call(
        _norm_kernel,
        grid=(m // tm,),
        in_specs=[pl.BlockSpec((tm, d), lambda i: (i, 0)), pl.BlockSpec((1, d), lambda i: (0, 0))],
        out_specs=pl.BlockSpec((tm, d), lambda i: (i, 0)),
        out_shape=jax.ShapeDtypeStruct((m, d), F32),
        compiler_params=_params("parallel"),
        name="final_norm",
    )(x, nw.reshape(1, d))


def _reorder_kernel(main_ref, tail_ref, o_ref, *, n_head, n_body):
    j = pl.program_id(1)
    tr = o_ref.shape[0]
    lr = tail_ref.shape[0]

    @pl.when(j < n_head)
    def _():
        o_ref[...] = main_ref[...].astype(BF16)

    @pl.when((j >= n_head) & (j < n_head + n_body))
    def _():
        o_ref[0:tr - lr, :] = main_ref[lr:tr, :].astype(BF16)
        o_ref[tr - lr:tr, :] = tail_ref[...].astype(BF16)

    @pl.when(j == n_head + n_body)
    def _():
        o_ref[0:lr, :] = tail_ref[...].astype(BF16)
        o_ref[lr:tr, :] = jnp.zeros((tr - lr, o_ref.shape[1]), BF16)


def reorder_w_in(wt, tr):
    nl, rows, d = wt.shape
    lr = LR_END - LR_START
    assert rows == IN_COLS and LR_START % tr == 0 and (IN_COLS - LR_END) % tr == 0 and tr % lr == 0
    n_head, n_body = LR_START // tr, (IN_COLS - LR_END) // tr
    assert (n_head + n_body + 1) * tr == PROJ_COLS
    per = tr // lr
    return pl.pallas_call(
        functools.partial(_reorder_kernel, n_head=n_head, n_body=n_body),
        grid=(nl, PROJ_COLS // tr),
        in_specs=[pl.BlockSpec((None, tr, d), lambda l, j: (l, jnp.minimum(j, n_head + n_body - 1), 0)),
                  pl.BlockSpec((None, lr, d), lambda l, j: (
                      l, jnp.where(j < n_head + n_body, per * (j + 1), LR_START // lr), 0))],
        out_specs=pl.BlockSpec((None, tr, d), lambda l, j: (l, j, 0)),
        out_shape=jax.ShapeDtypeStruct((nl, PROJ_COLS, d), BF16),
        compiler_params=_params("parallel", "parallel"),
        name="reorder_w_in",
    )(wt, wt)


def kernel(x, mem, norm_mix, w_in, gla_gk_w, gla_gk_b, gla_norm, hgrn_lb_logits, hgrn_norm, swa_sink, w_out, norm_mem_q, norm_mem_kv, mem_wq, mem_wkv, mem_wo, norm_ffn, router_w, expert_w_gate, expert_w_up, expert_w_down, norm_final):
    b, s, d = x.shape
    depth = w_in.shape[0]
    n_mem = mem.shape[1]
    cap = EC_CAPACITY * s // N_EXPERTS
    rope_t = rope_tables(s)
    sm = jax.nn.softmax(hgrn_lb_logits.astype(F32), axis=0)
    lower_bounds = jnp.clip(jnp.cumsum(sm, axis=0) - sm[0], 0.0, 1.0)
    mem2 = mem.reshape(b * n_mem, d)
    w_in_r = reorder_w_in(jnp.swapaxes(w_in, 1, 2), REORDER_ROWS)
    w_out_b, wq_b, wkv_b, wo_b = (w.astype(BF16) for w in (w_out, mem_wq, mem_wkv, mem_wo))
    rw_b = jnp.pad(router_w, ((0, 0), (0, 0), (0, LANES - N_EXPERTS))).astype(BF16)
    kv = norm_matmul_layers(mem2, norm_mem_kv, wkv_b, KV_PROJ_TN, BF16).reshape(depth, b, n_mem, -1)
    for l in range(depth):
        proj = norm_matmul(x.reshape(b * s, d), norm_mix[l], w_in_r, l, IN_PROJ_TM, IN_PROJ_TN, F32,
                           w_rows=True)
        proj = proj.reshape(b, s, PROJ_COLS)
        o_a = gla_group(proj, gla_gk_w[l], gla_gk_b[l], gla_norm[l])
        o_b = hgrn_group(proj, lower_bounds[l], hgrn_norm[l])
        o_c = swa_group(proj, swa_sink[l], rope_t)
        x, xn, aff = post_mixer(o_a, o_b, o_c, x, w_out_b, norm_mem_q[l], wq_b, kv, wo_b,
                                norm_ffn[l], rw_b, l, POST_MIXER_TM)
        aff_t = jnp.swapaxes(aff[:, :, :N_EXPERTS], 1, 2).reshape(b * N_EXPERTS, s)
        slot_t = select(aff_t, cap)
        xe, gate = gather(slot_t, aff_t, xn, cap)
        y = expert_ffn(xe, gate, expert_w_gate, expert_w_up, expert_w_down, l, FFN_TF, FFN_TN)
        slot = jnp.swapaxes(slot_t.reshape(b, N_EXPERTS, s), 1, 2)
        x = scatter(slot, y, x, SCATTER_TN)
    return final_norm(x.reshape(b * s, d), norm_final, FINAL_NORM_TM).reshape(b, s, d)
```

```python
import functools

import numpy as np
import jax
import jax.numpy as jnp
from jax import lax
from jax.experimental import pallas as pl
from jax.experimental.pallas import tpu as pltpu

F32 = jnp.float32
BF16 = jnp.bfloat16

EPS = 1e-6
NEG_BIG = -1e30
LANES = 128
SUBLANES = 8

GLA_HEADS = 4
GLA_DK = 64
GLA_DV = 128
GLA_RANK = 16
GLA_GATE_NORMALIZER = 16.0
HG_HEADS = 4
HG_DK = 128
HG_DV = 128
SWA_HEADS = 8
SWA_KV_HEADS = 2
SWA_GROUP = SWA_HEADS // SWA_KV_HEADS
SWA_HD = 128
SWA_WINDOW = 128
SWA_BLOCK = 128
SWA_Q_TILE = 512
ROPE_THETA = 500000.0
ROT_DIM = SWA_HD // 4
MEM_HEADS = 4
MEM_HD = 128
N_EXPERTS = 16
EC_CAPACITY = 2

COL_AQ, COL_AK, COL_AV, COL_AG = 0, 256, 512, 1024
COL_BQ, COL_BFF, COL_BFB, COL_BV, COL_BG = 1536, 2048, 2560, 3072, 3584
COL_CQ, COL_CK, COL_CV = 4096, 5120, 5376
COL_LR = 5632
PROJ_COLS = 6144
LR_START, LR_END, IN_COLS = 1536, 1568, 5664

SCAN_CHUNK = 128
SCAN_LEVELS = 7
SCAN_MM_LEVELS = 3
assert (1 << SCAN_LEVELS) == SCAN_CHUNK and (1 << SCAN_MM_LEVELS) == SUBLANES
VMEM_LIMIT = 56 * 1024 * 1024

IN_PROJ_TM, IN_PROJ_TN = 1024, 1024
KV_PROJ_TN = 512
POST_MIXER_TM = 512
FFN_TF, FFN_TN = 512, 512
SCATTER_TN = 512
REORDER_ROWS = 512
FINAL_NORM_TM = 1024

NT = (((1,), (1,)), ((), ()))
TN = (((0,), (0,)), ((), ()))


def _params(*sem):
    return pltpu.CompilerParams(dimension_semantics=sem, vmem_limit_bytes=VMEM_LIMIT)


def _rms(x, w):
    return x * lax.rsqrt(jnp.mean(x * x, axis=-1, keepdims=True) + EPS) * w


def _sigmoid(x):
    return 0.5 * jnp.tanh(0.5 * x) + 0.5


def _norm_matmul_kernel(x_ref, nw_ref, w_ref, o_ref, xn_ref, *, w_rows):
    @pl.when(pl.program_id(1) == 0)
    def _():
        xn_ref[...] = _rms(x_ref[...], nw_ref[...]).astype(BF16)

    dims = NT if w_rows else (((1,), (0,)), ((), ()))
    o_ref[...] = lax.dot_general(xn_ref[...], w_ref[...], dims,
                                 preferred_element_type=F32).astype(o_ref.dtype)


def norm_matmul(x, nw, w, layer, tm, tn, out_dtype, w_rows=False):
    m, k = x.shape
    n = w.shape[1] if w_rows else w.shape[2]
    w_spec = (pl.BlockSpec((None, tn, k), lambda i, j: (layer, j, 0)) if w_rows
              else pl.BlockSpec((None, k, tn), lambda i, j: (layer, 0, j)))
    return pl.pallas_call(
        functools.partial(_norm_matmul_kernel, w_rows=w_rows),
        grid=(m // tm, n // tn),
        in_specs=[pl.BlockSpec((tm, k), lambda i, j: (i, 0)),
                  pl.BlockSpec((1, k), lambda i, j: (0, 0)),
                  w_spec],
        out_specs=pl.BlockSpec((tm, tn), lambda i, j: (i, j)),
        out_shape=jax.ShapeDtypeStruct((m, n), out_dtype),
        scratch_shapes=[pltpu.VMEM((tm, k), BF16)],
        compiler_params=_params("parallel", "arbitrary"),
        name="norm_matmul",
    )(x, nw.reshape(1, k), w)


def norm_matmul_layers(x, nw, w, tn, out_dtype):
    m, k = x.shape
    nl, _, n = w.shape
    return pl.pallas_call(
        functools.partial(_norm_matmul_kernel, w_rows=False),
        grid=(nl, n // tn),
        in_specs=[pl.BlockSpec((m, k), lambda l, j: (0, 0)),
                  pl.BlockSpec((None, 1, k), lambda l, j: (l, 0, 0)),
                  pl.BlockSpec((None, k, tn), lambda l, j: (l, 0, j))],
        out_specs=pl.BlockSpec((None, m, tn), lambda l, j: (l, 0, j)),
        out_shape=jax.ShapeDtypeStruct((nl, m, n), out_dtype),
        scratch_shapes=[pltpu.VMEM((m, k), BF16)],
        compiler_params=_params("parallel", "arbitrary"),
        name="norm_matmul_layers",
    )(x, nw.reshape(nl, 1, k), w)


def _scan_constants(c, nlev, nmm):
    e = np.zeros((2, nmm - 1, c, c), np.float32)
    for l in range(1, nmm):
        m = 1 << l
        for i in range(c):
            start = (i // m) * m
            end = start + m - 1
            if (i // m) % 2 == 1:
                e[0, l - 1, i, start:i + 1] = 1
                e[1, l - 1, i, start:i] = 1
            else:
                e[0, l - 1, i, i + 1:end + 1] = 1
                e[1, l - 1, i, i:end + 1] = 1
    t = np.arange(c)
    x = t[:, None] ^ t[None, :]
    lv = np.where(x > 0, np.floor(np.log2(np.maximum(x, 1))), nlev).astype(np.int32)
    lvf = np.where(t[:, None] >= t[None, :], lv, -1)
    masks = np.stack([[lvd == l for l in range(nlev + 1)] for lvd in (lvf, lvf.T)]).astype(np.float32)
    return e.reshape(2, (nmm - 1) * c, c), masks


def _level_sums(b, l, rev):
    c = b.shape[0]
    m = 1 << l
    parts = []
    for j in range(c // (2 * m)):
        lo = 2 * m * j
        left, right = b[lo:lo + m], b[lo + m:lo + 2 * m]
        if rev:
            r = b[lo + m:lo + m + 1]
            parts += [left - r, r - right]
        else:
            r = b[lo + m - 1:lo + m]
            parts += [r - left, right - r]
    return jnp.concatenate(parts, axis=0)


def _running_sums(la, rev):
    c = la.shape[0]
    row = lax.broadcasted_iota(jnp.int32, (c, 1), 0)
    b = la
    sh = 1
    while sh < c:
        if rev:
            b = b + jnp.where(row < c - sh, pltpu.roll(b, c - sh, 0), 0.0)
        else:
            b = b + jnp.where(row >= sh, pltpu.roll(b, sh, 0), 0.0)
        sh *= 2
    return b


def _scan_chunk(q, k, vs, la, st_refs, e, mk_ref, qmasks, rev):
    c, nlev, nmm = SCAN_CHUNK, SCAN_LEVELS, SCAN_MM_LEVELS
    b = _running_sums(la, rev)
    x = jnp.dot(e, la.astype(BF16), preferred_element_type=F32)
    odd = (lax.broadcasted_iota(jnp.int32, (c, 1), 0) & 1) == 1
    x0 = jnp.where(~odd if rev else odd, la, 0.0)
    tot = b[0:1] if rev else b[c - 1:c]
    qms = [(q * qm).astype(BF16) for qm in qmasks]
    kb = k.astype(BF16)
    acc = [lax.dot_general(qm, kb, NT, preferred_element_type=F32).astype(BF16) * mk_ref[nlev]
           for qm in qms]
    for l in range(nlev):
        if l == 0:
            xl = x0
        elif l < nmm:
            xl = x[(l - 1) * c:l * c]
        else:
            xl = _level_sums(b, l, rev)
        fl = jnp.exp(xl).astype(BF16)
        ksl = kb * fl
        al = lax.dot_general(jnp.concatenate([qm * fl for qm in qms], axis=0), ksl, NT,
                             preferred_element_type=F32)
        for h in range(len(qms)):
            acc[h] = acc[h] + al[h * c:(h + 1) * c].astype(BF16) * mk_ref[l]
    fq = jnp.exp(b).astype(BF16)
    ksk = kb * jnp.exp(tot - b).astype(BF16)
    decay = jnp.exp(tot)
    outs = []
    for h, qm in enumerate(qms):
        st = st_refs[h][...]
        o = jnp.dot(acc[h], vs[h], preferred_element_type=F32)
        o = o + lax.dot_general(qm * fq, st.astype(BF16), NT, preferred_element_type=F32)
        st_refs[h][...] = st * decay + lax.dot_general(vs[h], ksk, TN, preferred_element_type=F32)
        outs.append(o)
    return outs


def _scan_driver(prologue, qmasks, g_ref, nw_ref, e_ref, mk_ref, o_ref, of_ref, ob_ref, st_ref):
    c = SCAN_CHUNK
    s = of_ref.shape[0]
    nc = s // c
    nh = len(qmasks)
    st_ref[...] = jnp.zeros_like(st_ref)

    def one(r, d, dst_ref):
        q, k, v, la = prologue(r, d)
        vs = [v[:, h * LANES:(h + 1) * LANES] for h in range(nh)]
        outs = _scan_chunk(q, k, vs, la, [st_ref.at[d * nh + h] for h in range(nh)],
                           e_ref[d], mk_ref.at[d], qmasks, bool(d))
        for h in range(nh):
            dst_ref[pl.ds(r, c), h * LANES:(h + 1) * LANES] = outs[h]

    def body(n, carry):
        one(pl.multiple_of(n * c, c), 0, of_ref)
        one(pl.multiple_of((nc - 1 - n) * c, c), 1, ob_ref)
        return carry

    lax.fori_loop(0, nc, body, 0, unroll=2)

    def finish(n, carry):
        r = pl.multiple_of(n * c, c)
        for h in range(nh):
            cols = slice(h * LANES, (h + 1) * LANES)
            o = of_ref[pl.ds(r, c), cols] + ob_ref[pl.ds(r, c), cols]
            g = g_ref[0, pl.ds(r, c), cols]
            o_ref[0, pl.ds(r, c), cols] = (_rms(o, nw_ref[...]) * (g * _sigmoid(g))).astype(o_ref.dtype)
        return carry

    lax.fori_loop(0, nc, finish, 0)


def _gla_kernel(q_ref, k_ref, v_ref, g_ref, lr_ref, gw_ref, gb_ref, nw_ref, e_ref, mk_ref,
                o_ref, of_ref, ob_ref, st_ref, la_ref):
    c = SCAN_CHUNK
    lane = lax.broadcasted_iota(jnp.int32, (1, LANES), 1)
    qmasks = [jnp.where(lane // GLA_DK == h, GLA_DK ** -0.5, 0.0).astype(F32)
              for h in range(LANES // GLA_DK)]
    lr = lr_ref[0].astype(BF16)
    for d in range(2):
        gk = jnp.dot(lr, gw_ref[d], preferred_element_type=F32) + gb_ref[d]
        la_ref[d] = (jnp.minimum(gk, 0.0) - jnp.log(1.0 + jnp.exp(-jnp.abs(gk)))) * (
            1.0 / GLA_GATE_NORMALIZER)

    def prologue(r, d):
        q = q_ref[0, pl.ds(r, c), :]
        k = k_ref[0, pl.ds(r, c), :]
        v = v_ref[0, pl.ds(r, c), :].astype(BF16)
        return q, k, v, la_ref[d, pl.ds(r, c), :]

    _scan_driver(prologue, qmasks, g_ref, nw_ref, e_ref, mk_ref, o_ref, of_ref, ob_ref, st_ref)


def _hgrn_kernel(q_ref, ff_ref, fb_ref, v_ref, g_ref, lb_ref, nw_ref, e_ref, mk_ref,
                 o_ref, of_ref, ob_ref, st_ref):
    c = SCAN_CHUNK
    qmasks = [jnp.full((1, LANES), HG_DK ** -0.5, F32)]

    def prologue(r, d):
        q = q_ref[0, pl.ds(r, c), :]
        q = q * _sigmoid(q)
        z = (fb_ref if d else ff_ref)[0, pl.ds(r, c), :]
        lb = lb_ref[d]
        f = jnp.clip(lb + (1.0 - lb) * _sigmoid(z), 1e-6, 1.0)
        v = v_ref[0, pl.ds(r, c), :].astype(BF16)
        return q, 1.0 - f, v, jnp.log(f)

    _scan_driver(prologue, qmasks, g_ref, nw_ref, e_ref, mk_ref, o_ref, of_ref, ob_ref, st_ref)


def _scan_call(kernel, name, proj, cols, extra, extra_specs, norm_w, steps, heads_per_step,
               more_scratch=()):
    b, s, _ = proj.shape
    e_np, mk_np = _scan_constants(SCAN_CHUNK, SCAN_LEVELS, SCAN_MM_LEVELS)
    e = jnp.asarray(e_np, BF16)
    mk = jnp.asarray(mk_np, BF16)
    ow = heads_per_step * LANES

    def col_spec(off, width):
        return pl.BlockSpec((1, s, width), lambda bi, h: (bi, 0, off // width + h))

    in_specs = [col_spec(off, width) for off, width in cols] + extra_specs + [
        pl.BlockSpec((1, LANES), lambda bi, h: (0, 0)),
        pl.BlockSpec(e.shape, lambda bi, h: (0, 0, 0)),
        pl.BlockSpec(mk.shape, lambda bi, h: (0, 0, 0, 0))]
    return pl.pallas_call(
        kernel,
        grid=(b, steps),
        in_specs=in_specs,
        out_specs=pl.BlockSpec((1, s, ow), lambda bi, h: (bi, 0, h)),
        out_shape=jax.ShapeDtypeStruct((b, s, steps * ow), BF16),
        scratch_shapes=[pltpu.VMEM((s, ow), F32), pltpu.VMEM((s, ow), F32),
                        pltpu.VMEM((2 * heads_per_step, LANES, LANES), F32)] + list(more_scratch),
        compiler_params=_params("parallel", "parallel"),
        name=name,
    )(*([proj] * len(cols)), *extra, norm_w.reshape(1, LANES), e, mk)


def gla_group(proj, gk_w, gk_b, norm_w):
    gw = jnp.zeros((2, LANES, GLA_HEADS * GLA_DK), F32)
    gw = gw.at[0, 0:GLA_RANK].set(gk_w[0]).at[1, GLA_RANK:2 * GLA_RANK].set(gk_w[1]).astype(BF16)
    gb = gk_b.reshape(2, 1, GLA_HEADS * GLA_DK)
    pair = LANES // GLA_DK
    cols = [(COL_AQ, LANES), (COL_AK, LANES), (COL_AV, pair * GLA_DV), (COL_AG, pair * GLA_DV)]
    extra_specs = [pl.BlockSpec((1, proj.shape[1], LANES), lambda bi, h: (bi, 0, COL_LR // LANES)),
                   pl.BlockSpec((2, LANES, LANES), lambda bi, h: (0, 0, h)),
                   pl.BlockSpec((2, 1, LANES), lambda bi, h: (0, 0, h))]
    return _scan_call(_gla_kernel, "gla_scan", proj, cols, [proj, gw, gb], extra_specs, norm_w,
                      GLA_HEADS // pair, pair,
                      more_scratch=[pltpu.VMEM((2, proj.shape[1], LANES), F32)])


def hgrn_group(proj, lb, norm_w):
    cols = [(COL_BQ, LANES), (COL_BFF, LANES), (COL_BFB, LANES), (COL_BV, LANES), (COL_BG, LANES)]
    extra_specs = [pl.BlockSpec((2, 1, LANES), lambda bi, h: (0, 0, h))]
    return _scan_call(_hgrn_kernel, "hgrn_scan", proj, cols, [lb.reshape(2, 1, HG_HEADS * HG_DK)],
                      extra_specs, norm_w, HG_HEADS, 1)


def _rope(x, cos, sin_hi, sin_lo):
    return (x * cos + pltpu.roll(x, LANES - ROT_DIM // 2, 1) * sin_hi
            + pltpu.roll(x, ROT_DIM // 2, 1) * sin_lo)


def _swa_kernel(sink_ref, q_ref, k_ref, v_ref, t_ref, o_ref, kr_ref, vb_ref):
    kv = pl.program_id(1)
    n = pl.program_id(2)
    blk = SWA_BLOCK
    s = k_ref.shape[1]
    nb = s // blk
    nq = q_ref.shape[1] // blk
    rows = SWA_GROUP * blk

    @pl.when(n == 0)
    def _():
        zero = jnp.zeros((blk, SWA_HD), BF16)
        for ref in (kr_ref, vb_ref):
            ref[0:blk, :] = zero
            ref[blk + s:2 * blk + s, :] = zero
        for j in range(nb):
            r = slice(j * blk, (j + 1) * blk)
            kr_ref[blk + j * blk:2 * blk + j * blk, :] = _rope(
                k_ref[0, r, :], t_ref[0, r, :], t_ref[1, r, :], t_ref[2, r, :]).astype(BF16)
            vb_ref[blk + j * blk:2 * blk + j * blk, :] = v_ref[0, r, :].astype(BF16)

    qi = lax.broadcasted_iota(jnp.int32, (rows, 3 * blk), 0) & (blk - 1)
    ki = lax.broadcasted_iota(jnp.int32, (rows, 3 * blk), 1)
    window_bias = jnp.where(jnp.abs(ki - blk - qi) <= SWA_WINDOW, 0.0, NEG_BIG)
    col_block = lax.broadcasted_iota(jnp.int32, (1, 3 * blk), 1) // blk
    head = lax.broadcasted_iota(jnp.int32, (rows, 1), 0) // blk
    sink = jnp.zeros((rows, 1), F32)
    for g in range(SWA_GROUP):
        sink = jnp.where(head == g, sink_ref[0, kv * SWA_GROUP + g], sink)
    for j in range(nq):
        qb = n * nq + j
        r0 = pl.multiple_of(qb * blk, blk)
        cos, sin_hi, sin_lo = (t_ref[i, pl.ds(r0, blk), :] * (SWA_HD ** -0.5) for i in range(3))
        q4 = jnp.concatenate(
            [_rope(q_ref[0, j * blk:(j + 1) * blk, g * SWA_HD:(g + 1) * SWA_HD], cos, sin_hi, sin_lo)
             for g in range(SWA_GROUP)], axis=0).astype(BF16)
        kb = kr_ref[pl.ds(r0, 3 * blk), :]
        vb = vb_ref[pl.ds(r0, 3 * blk), :]
        logits = lax.dot_general(q4, kb, NT, preferred_element_type=F32)
        off_seq = ((col_block == 0) & (qb == 0)) | ((col_block == 2) & (qb == nb - 1))
        logits = logits + window_bias + jnp.where(off_seq, NEG_BIG, 0.0)
        m = jnp.maximum(jnp.max(logits, axis=-1, keepdims=True), sink)
        p = jnp.exp(logits - m)
        denom = jnp.sum(p, axis=-1, keepdims=True) + jnp.exp(sink - m)
        o = jnp.dot((p / denom).astype(BF16), vb, preferred_element_type=F32)
        for g in range(SWA_GROUP):
            o_ref[0, j * blk:(j + 1) * blk, g * SWA_HD:(g + 1) * SWA_HD] = (
                o[g * blk:(g + 1) * blk].astype(o_ref.dtype))


def swa_group(proj, sink, rope_t):
    b, s, _ = proj.shape
    gw = SWA_GROUP * SWA_HD
    tq = SWA_Q_TILE

    def kv_spec(col):
        return pl.BlockSpec((1, s, SWA_HD), lambda bi, kv, n: (bi, 0, col // SWA_HD + kv))

    return pl.pallas_call(
        _swa_kernel,
        grid=(b, SWA_KV_HEADS, s // tq),
        in_specs=[pl.BlockSpec(memory_space=pltpu.SMEM),
                  pl.BlockSpec((1, tq, gw), lambda bi, kv, n: (bi, n, COL_CQ // gw + kv)),
                  kv_spec(COL_CK), kv_spec(COL_CV),
                  pl.BlockSpec(rope_t.shape, lambda bi, kv, n: (0, 0, 0))],
        out_specs=pl.BlockSpec((1, tq, gw), lambda bi, kv, n: (bi, n, kv)),
        out_shape=jax.ShapeDtypeStruct((b, s, SWA_HEADS * SWA_HD), BF16),
        scratch_shapes=[pltpu.VMEM((s + 2 * SWA_BLOCK, SWA_HD), BF16),
                        pltpu.VMEM((s + 2 * SWA_BLOCK, SWA_HD), BF16)],
        compiler_params=_params("parallel", "parallel", "arbitrary"),
        name="swa",
    )(sink.reshape(1, SWA_HEADS), proj, proj, proj, rope_t)


def rope_tables(s):
    half = ROT_DIM // 2
    inv = ROPE_THETA ** (-jnp.arange(0, ROT_DIM, 2, dtype=F32) / ROT_DIM)
    ang = jnp.arange(s, dtype=F32)[:, None] * inv[None, :]
    cos, sin = jnp.cos(ang), jnp.sin(ang)
    z = jnp.zeros((s, LANES - ROT_DIM), F32)
    zh = jnp.zeros((s, half), F32)
    return jnp.stack([jnp.concatenate([cos, cos, z + 1.0], axis=1),
                      jnp.concatenate([-sin, zh, z], axis=1),
                      jnp.concatenate([zh, sin, z], axis=1)])


def _post_mixer_kernel(oa_ref, ob_ref, oc_ref, x_ref, wout_ref, nq_ref, wq_ref, kv_ref, wo_ref,
                       nf_ref, rw_ref, xo_ref, xn_ref, aff_ref):
    ka, kb = oa_ref.shape[-1], ob_ref.shape[-1]
    x = x_ref[0]
    x = x + jnp.dot(oa_ref[0], wout_ref[0:ka, :], preferred_element_type=F32)
    x = x + jnp.dot(ob_ref[0], wout_ref[ka:ka + kb, :], preferred_element_type=F32)
    x = x + jnp.dot(oc_ref[0], wout_ref[ka + kb:, :], preferred_element_type=F32)
    q = jnp.dot(_rms(x, nq_ref[...]).astype(BF16), wq_ref[...], preferred_element_type=F32).astype(BF16)
    w = MEM_HEADS * MEM_HD
    outs = []
    for h in range(MEM_HEADS):
        k = kv_ref[0, :, h * MEM_HD:(h + 1) * MEM_HD]
        v = kv_ref[0, :, w + h * MEM_HD:w + (h + 1) * MEM_HD]
        logits = lax.dot_general(q[:, h * MEM_HD:(h + 1) * MEM_HD], k, NT,
                                 preferred_element_type=F32) * (MEM_HD ** -0.5)
        p = jnp.exp(logits - jnp.max(logits, axis=-1, keepdims=True))
        o = jnp.dot(p.astype(BF16), v, preferred_element_type=F32) / jnp.sum(p, axis=-1, keepdims=True)
        outs.append(o.astype(BF16))
    x = x + jnp.dot(jnp.concatenate(outs, axis=-1), wo_ref[...], preferred_element_type=F32)
    xo_ref[0] = x
    xn = _rms(x, nf_ref[...]).astype(BF16)
    xn_ref[0] = xn
    logits = jnp.dot(xn, rw_ref[...], preferred_element_type=F32)
    lane = lax.broadcasted_iota(jnp.int32, logits.shape, 1)
    logits = jnp.where(lane < N_EXPERTS, logits, NEG_BIG)
    p = jnp.exp(logits - jnp.max(logits, axis=-1, keepdims=True))
    aff_ref[0] = p / jnp.sum(p, axis=-1, keepdims=True)


def post_mixer(oa, ob, oc, x, w_out, nq, wq, kv, wo, nf, rw, layer, tm):
    b, s, d = x.shape
    n_mem, kvw = kv.shape[2], kv.shape[3]

    def rows(width):
        return pl.BlockSpec((1, tm, width), lambda bi, i: (bi, i, 0))

    def whole(a):
        return pl.BlockSpec((None,) + a.shape[1:], lambda bi, i: (layer, 0, 0),
                            pipeline_mode=pl.Buffered(1))

    vec = pl.BlockSpec((1, d), lambda bi, i: (0, 0))
    return pl.pallas_call(
        _post_mixer_kernel,
        grid=(b, s // tm),
        in_specs=[rows(oa.shape[2]), rows(ob.shape[2]), rows(oc.shape[2]), rows(d), whole(w_out),
                  vec, whole(wq),
                  pl.BlockSpec((None, 1, n_mem, kvw), lambda bi, i: (layer, bi, 0, 0)), whole(wo),
                  vec, whole(rw)],
        out_specs=[rows(d), rows(d), rows(LANES)],
        out_shape=[jax.ShapeDtypeStruct((b, s, d), F32), jax.ShapeDtypeStruct((b, s, d), BF16),
                   jax.ShapeDtypeStruct((b, s, LANES), F32)],
        compiler_params=_params("parallel", "parallel"),
        name="post_mixer",
    )(oa, ob, oc, x, w_out, nq.reshape(1, d), wq, kv, wo, nf.reshape(1, d), rw)


def _select_kernel(aff_ref, tri_ref, slot_ref, *, cap):
    aff = aff_ref[...]
    key = pltpu.bitcast(aff, jnp.int32)

    def bit_step(i, lo):
        cand = lo | (1 << (30 - i))
        cnt = jnp.sum(jnp.where(key >= cand, 1.0, 0.0), axis=-1, keepdims=True)
        return jnp.where(cnt >= cap, cand, lo)

    thr = lax.fori_loop(0, 31, bit_step, jnp.zeros((aff.shape[0], 1), jnp.int32))
    gt = key > thr
    eq = key == thr
    need = cap - jnp.sum(jnp.where(gt, 1.0, 0.0), axis=-1, keepdims=True)
    tri = tri_ref[...]
    eq_rank = jnp.dot(jnp.where(eq, 1.0, 0.0).astype(BF16), tri, preferred_element_type=F32)
    sel = gt | (eq & (eq_rank <= need))
    pos = jnp.dot(jnp.where(sel, 1.0, 0.0).astype(BF16), tri, preferred_element_type=F32)
    slot_ref[...] = jnp.where(sel, pos.astype(jnp.int32) - 1, -1)


def select(aff_t, cap):
    r, s = aff_t.shape
    tri = jnp.triu(jnp.ones((s, s), BF16))
    return pl.pallas_call(
        functools.partial(_select_kernel, cap=cap),
        grid=(1,),
        in_specs=[pl.BlockSpec((r, s), lambda i: (0, 0)), pl.BlockSpec((s, s), lambda i: (0, 0))],
        out_specs=pl.BlockSpec((r, s), lambda i: (0, 0)),
        out_shape=jax.ShapeDtypeStruct((r, s), jnp.int32),
        compiler_params=_params("arbitrary"),
        name="select",
    )(aff_t, tri)


def _gather_kernel(slot_ref, aff_ref, xn_ref, xe_ref, gate_ref, *, cap):
    slot = slot_ref[0]
    hit = lax.broadcasted_iota(jnp.int32, (cap, slot.shape[1]), 0) == slot
    xe_ref[0, 0] = jnp.dot(jnp.where(hit, 1.0, 0.0).astype(BF16), xn_ref[0],
                           preferred_element_type=F32).astype(BF16)
    gate_ref[0, 0] = jnp.sum(jnp.where(hit, aff_ref[0], 0.0), axis=-1, keepdims=True)


def gather(slot_t, aff_t, xn, cap):
    b, s, d = xn.shape
    ne = slot_t.shape[0] // b
    return pl.pallas_call(
        functools.partial(_gather_kernel, cap=cap),
        grid=(b, ne),
        in_specs=[pl.BlockSpec((1, 1, s), lambda bi, e: (bi * ne + e, 0, 0)),
                  pl.BlockSpec((1, 1, s), lambda bi, e: (bi * ne + e, 0, 0)),
                  pl.BlockSpec((1, s, d), lambda bi, e: (bi, 0, 0))],
        out_specs=[pl.BlockSpec((1, 1, cap, d), lambda bi, e: (bi, e, 0, 0)),
                   pl.BlockSpec((1, 1, cap, 1), lambda bi, e: (bi, e, 0, 0))],
        out_shape=[jax.ShapeDtypeStruct((b, ne, cap, d), BF16),
                   jax.ShapeDtypeStruct((b, ne, cap, 1), F32)],
        compiler_params=_params("parallel", "arbitrary"),
        name="moe_gather",
    )(slot_t.reshape(b * ne, 1, s), aff_t.reshape(b * ne, 1, s), xn)


def _ffn_kernel(xe_ref, gate_ref, wg_ref, wu_ref, wd_ref, y_ref, h_ref):
    st = pl.program_id(1)
    nf, rows, tf = h_ref.shape
    d = xe_ref.shape[-1]

    @pl.when(st < nf)
    def _():
        xe = xe_ref[...].reshape(rows, d)
        hg = jnp.dot(xe, wg_ref[0].astype(BF16), preferred_element_type=F32)
        hu = jnp.dot(xe, wu_ref[0].astype(BF16), preferred_element_type=F32)
        h_ref[st] = (hg * _sigmoid(hg) * hu * gate_ref[...].reshape(rows, 1)).astype(BF16)

    @pl.when(st >= nf)
    def _():
        y = jnp.dot(h_ref[0], wd_ref[0, 0:tf, :].astype(BF16), preferred_element_type=F32)
        for f in range(1, nf):
            y = y + jnp.dot(h_ref[f], wd_ref[0, f * tf:(f + 1) * tf, :].astype(BF16),
                            preferred_element_type=F32)
        y_ref[...] = y.reshape(y_ref.shape).astype(y_ref.dtype)


def expert_ffn(xe, gate, wg, wu, wd, layer, tf, tn):
    b, ne, cap, d = xe.shape
    dexp = wg.shape[3]
    nf = dexp // tf
    return pl.pallas_call(
        _ffn_kernel,
        grid=(ne, nf + d // tn),
        in_specs=[pl.BlockSpec((b, 1, cap, d), lambda e, st: (0, e, 0, 0)),
                  pl.BlockSpec((b, 1, cap, 1), lambda e, st: (0, e, 0, 0)),
                  pl.BlockSpec((None, 1, d, tf), lambda e, st: (layer, e, 0, jnp.minimum(st, nf - 1))),
                  pl.BlockSpec((None, 1, d, tf), lambda e, st: (layer, e, 0, jnp.minimum(st, nf - 1))),
                  pl.BlockSpec((None, 1, dexp, tn), lambda e, st: (layer, e, 0, jnp.maximum(st - nf, 0)))],
        out_specs=pl.BlockSpec((b, 1, cap, tn), lambda e, st: (0, e, 0, jnp.maximum(st - nf, 0))),
        out_shape=jax.ShapeDtypeStruct((b, ne, cap, d), BF16),
        scratch_shapes=[pltpu.VMEM((nf, b * cap, tf), BF16)],
        compiler_params=_params("parallel", "arbitrary"),
        name="expert_ffn",
    )(xe, gate, wg, wu, wd)


def _scatter_kernel(slot_ref, y_ref, x_ref, o_ref, *, cap):
    ne = y_ref.shape[1]
    s = x_ref.shape[1]
    lane = lax.broadcasted_iota(jnp.int32, (s, cap), 1)
    acc = x_ref[0]
    for e in range(ne):
        hit = slot_ref[0, :, e:e + 1] == lane
        acc = acc + jnp.dot(jnp.where(hit, 1.0, 0.0).astype(BF16), y_ref[0, e],
                            preferred_element_type=F32)
    o_ref[0] = acc


def scatter(slot, y, x, tn):
    b, s, d = x.shape
    ne, cap = y.shape[1], y.shape[2]
    return pl.pallas_call(
        functools.partial(_scatter_kernel, cap=cap),
        grid=(b, d // tn),
        in_specs=[pl.BlockSpec((1, s, ne), lambda bi, j: (bi, 0, 0)),
                  pl.BlockSpec((1, ne, cap, tn), lambda bi, j: (bi, 0, 0, j)),
                  pl.BlockSpec((1, s, tn), lambda bi, j: (bi, 0, j))],
        out_specs=pl.BlockSpec((1, s, tn), lambda bi, j: (bi, 0, j)),
        out_shape=jax.ShapeDtypeStruct((b, s, d), F32),
        compiler_params=_params("parallel", "parallel"),
        name="moe_scatter",
    )(slot, y, x)


def _norm_kernel(x_ref, nw_ref, o_ref):
    o_ref[...] = _rms(x_ref[...], nw_ref[...])


def final_norm(x, nw, tm):
    m, d = x.shape
    return pl.pallas_call(
        _norm_kernel,
        grid=(m // tm,),
        in_specs=[pl.BlockSpec((tm, d), lambda i: (i, 0)), pl.BlockSpec((1, d), lambda i: (0, 0))],
        out_specs=pl.BlockSpec((tm, d), lambda i: (i, 0)),
        out_shape=jax.ShapeDtypeStruct((m, d), F32),
        compiler_params=_params("parallel"),
        name="final_norm",
    )(x, nw.reshape(1, d))


def _reorder_kernel(main_ref, tail_ref, o_ref, *, n_head, n_body):
    j = pl.program_id(1)
    tr = o_ref.shape[0]
    lr = tail_ref.shape[0]

    @pl.when(j < n_head)
    def _():
        o_ref[...] = main_ref[...].astype(BF16)

    @pl.when((j >= n_head) & (j < n_head + n_body))
    def _():
        o_ref[0:tr - lr, :] = main_ref[lr:tr, :].astype(BF16)
        o_ref[tr - lr:tr, :] = tail_ref[...].astype(BF16)

    @pl.when(j == n_head + n_body)
    def _():
        o_ref[0:lr, :] = tail_ref[...].astype(BF16)
        o_ref[lr:tr, :] = jnp.zeros((tr - lr, o_ref.shape[1]), BF16)


def reorder_w_in(wt, tr):
    nl, rows, d = wt.shape
    lr = LR_END - LR_START
    assert rows == IN_COLS and LR_START % tr == 0 and (IN_COLS - LR_END) % tr == 0 and tr % lr == 0
    n_head, n_body = LR_START // tr, (IN_COLS - LR_END) // tr
    assert (n_head + n_body + 1) * tr == PROJ_COLS
    per = tr // lr
    return pl.pallas_call(
        functools.partial(_reorder_kernel, n_head=n_head, n_body=n_body),
        grid=(nl, PROJ_COLS // tr),
        in_specs=[pl.BlockSpec((None, tr, d), lambda l, j: (l, jnp.minimum(j, n_head + n_body - 1), 0)),
                  pl.BlockSpec((None, lr, d), lambda l, j: (
                      l, jnp.where(j < n_head + n_body, per * (j + 1), LR_START // lr), 0))],
        out_specs=pl.BlockSpec((None, tr, d), lambda l, j: (l, j, 0)),
        out_shape=jax.ShapeDtypeStruct((nl, PROJ_COLS, d), BF16),
        compiler_params=_params("parallel", "parallel"),
        name="reorder_w_in",
    )(wt, wt)


def kernel(x, mem, norm_mix, w_in, gla_gk_w, gla_gk_b, gla_norm, hgrn_lb_logits, hgrn_norm, swa_sink, w_out, norm_mem_q, norm_mem_kv, mem_wq, mem_wkv, mem_wo, norm_ffn, router_w, expert_w_gate, expert_w_up, expert_w_down, norm_final):
    b, s, d = x.shape
    depth = w_in.shape[0]
    n_mem = mem.shape[1]
    cap = EC_CAPACITY * s // N_EXPERTS
    rope_t = rope_tables(s)
    sm = jax.nn.softmax(hgrn_lb_logits.astype(F32), axis=0)
    lower_bounds = jnp.clip(jnp.cumsum(sm, axis=0) - sm[0], 0.0, 1.0)
    mem2 = mem.reshape(b * n_mem, d)
    w_in_r = reorder_w_in(jnp.swapaxes(w_in, 1, 2), REORDER_ROWS)
    w_out_b, wq_b, wkv_b, wo_b = (w.astype(BF16) for w in (w_out, mem_wq, mem_wkv, mem_wo))
    rw_b = jnp.pad(router_w, ((0, 0), (0, 0), (0, LANES - N_EXPERTS))).astype(BF16)
    kv = norm_matmul_layers(mem2, norm_mem_kv, wkv_b, KV_PROJ_TN, BF16).reshape(depth, b, n_mem, -1)
    for l in range(depth):
        proj = norm_matmul(x.reshape(b * s, d), norm_mix[l], w_in_r, l, IN_PROJ_TM, IN_PROJ_TN, F32,
                           w_rows=True)
        proj = proj.reshape(b, s, PROJ_COLS)
        o_a = gla_group(proj, gla_gk_w[l], gla_gk_b[l], gla_norm[l])
        o_b = hgrn_group(proj, lower_bounds[l], hgrn_norm[l])
        o_c = swa_group(proj, swa_sink[l], rope_t)
        x, xn, aff = post_mixer(o_a, o_b, o_c, x, w_out_b, norm_mem_q[l], wq_b, kv, wo_b,
                                norm_ffn[l], rw_b, l, POST_MIXER_TM)
        aff_t = jnp.swapaxes(aff[:, :, :N_EXPERTS], 1, 2).reshape(b * N_EXPERTS, s)
        slot_t = select(aff_t, cap)
        xe, gate = gather(slot_t, aff_t, xn, cap)
        y = expert_ffn(xe, gate, expert_w_gate, expert_w_up, expert_w_down, l, FFN_TF, FFN_TN)
        slot = jnp.swapaxes(slot_t.reshape(b, N_EXPERTS, s), 1, 2)
        x = scatter(slot, y, x, SCATTER_TN)
    return final_norm(x.reshape(b * s, d), norm_final, FINAL_NORM_TM).reshape(b, s, d)
```

```python
import functools

import numpy as np
import jax
import jax.numpy as jnp
from jax import lax
from jax.experimental import pallas as pl
from jax.experimental.pallas import tpu as pltpu

F32 = jnp.float32
BF16 = jnp.bfloat16

EPS = 1e-6
NEG_BIG = -1e30
LANES = 128
SUBLANES = 8

GLA_HEADS = 4
GLA_DK = 64
GLA_DV = 128
GLA_RANK = 16
GLA_GATE_NORMALIZER = 16.0
HG_HEADS = 4
HG_DK = 128
HG_DV = 128
SWA_HEADS = 8
SWA_KV_HEADS = 2
SWA_GROUP = SWA_HEADS // SWA_KV_HEADS
SWA_HD = 128
SWA_WINDOW = 128
SWA_BLOCK = 128
SWA_Q_TILE = 512
ROPE_THETA = 500000.0
ROT_DIM = SWA_HD // 4
MEM_HEADS = 4
MEM_HD = 128
N_EXPERTS = 16
EC_CAPACITY = 2

COL_AQ, COL_AK, COL_AV, COL_AG = 0, 256, 512, 1024
COL_BQ, COL_BFF, COL_BFB, COL_BV, COL_BG = 1536, 2048, 2560, 3072, 3584
COL_CQ, COL_CK, COL_CV = 4096, 5120, 5376
COL_LR = 5632
PROJ_COLS = 6144
LR_START, LR_END, IN_COLS = 1536, 1568, 5664

SCAN_CHUNK = 128
SCAN_LEVELS = 7
SCAN_MM_LEVELS = 3
assert (1 << SCAN_LEVELS) == SCAN_CHUNK and (1 << SCAN_MM_LEVELS) == SUBLANES
VMEM_LIMIT = 56 * 1024 * 1024

IN_PROJ_TM, IN_PROJ_TN = 1024, 1024
KV_PROJ_TN = 512
POST_MIXER_TM = 512
FFN_TF, FFN_TN = 512, 512
SCATTER_TN = 512
REORDER_ROWS = 512
FINAL_NORM_TM = 1024

NT = (((1,), (1,)), ((), ()))
TN = (((0,), (0,)), ((), ()))


def _params(*sem):
    return pltpu.CompilerParams(dimension_semantics=sem, vmem_limit_bytes=VMEM_LIMIT)


def _rms(x, w):
    return x * lax.rsqrt(jnp.mean(x * x, axis=-1, keepdims=True) + EPS) * w


def _sigmoid(x):
    return 0.5 * jnp.tanh(0.5 * x) + 0.5


def _norm_matmul_kernel(x_ref, nw_ref, w_ref, o_ref, xn_ref, *, w_rows):
    @pl.when(pl.program_id(1) == 0)
    def _():
        xn_ref[...] = _rms(x_ref[...], nw_ref[...]).astype(BF16)

    dims = NT if w_rows else (((1,), (0,)), ((), ()))
    o_ref[...] = lax.dot_general(xn_ref[...], w_ref[...], dims,
                                 preferred_element_type=F32).astype(o_ref.dtype)


def norm_matmul(x, nw, w, layer, tm, tn, out_dtype, w_rows=False):
    m, k = x.shape
    n = w.shape[1] if w_rows else w.shape[2]
    w_spec = (pl.BlockSpec((None, tn, k), lambda i, j: (layer, j, 0)) if w_rows
              else pl.BlockSpec((None, k, tn), lambda i, j: (layer, 0, j)))
    return pl.pallas_call(
        functools.partial(_norm_matmul_kernel, w_rows=w_rows),
        grid=(m // tm, n // tn),
        in_specs=[pl.BlockSpec((tm, k), lambda i, j: (i, 0)),
                  pl.BlockSpec((1, k), lambda i, j: (0, 0)),
                  w_spec],
        out_specs=pl.BlockSpec((tm, tn), lambda i, j: (i, j)),
        out_shape=jax.ShapeDtypeStruct((m, n), out_dtype),
        scratch_shapes=[pltpu.VMEM((tm, k), BF16)],
        compiler_params=_params("parallel", "arbitrary"),
        name="norm_matmul",
    )(x, nw.reshape(1, k), w)


def norm_matmul_layers(x, nw, w, tn, out_dtype):
    m, k = x.shape
    nl, _, n = w.shape
    return pl.pallas_call(
        functools.partial(_norm_matmul_kernel, w_rows=False),
        grid=(nl, n // tn),
        in_specs=[pl.BlockSpec((m, k), lambda l, j: (0, 0)),
                  pl.BlockSpec((None, 1, k), lambda l, j: (l, 0, 0)),
                  pl.BlockSpec((None, k, tn), lambda l, j: (l, 0, j))],
        out_specs=pl.BlockSpec((None, m, tn), lambda l, j: (l, 0, j)),
        out_shape=jax.ShapeDtypeStruct((nl, m, n), out_dtype),
        scratch_shapes=[pltpu.VMEM((m, k), BF16)],
        compiler_params=_params("parallel", "arbitrary"),
        name="norm_matmul_layers",
    )(x, nw.reshape(nl, 1, k), w)


def _scan_constants(c, nlev, nmm):
    e = np.zeros((2, nmm - 1, c, c), np.float32)
    for l in range(1, nmm):
        m = 1 << l
        for i in range(c):
            start = (i // m) * m
            end = start + m - 1
            if (i // m) % 2 == 1:
                e[0, l - 1, i, start:i + 1] = 1
                e[1, l - 1, i, start:i] = 1
            else:
                e[0, l - 1, i, i + 1:end + 1] = 1
                e[1, l - 1, i, i:end + 1] = 1
    t = np.arange(c)
    x = t[:, None] ^ t[None, :]
    lv = np.where(x > 0, np.floor(np.log2(np.maximum(x, 1))), nlev).astype(np.int32)
    lvf = np.where(t[:, None] >= t[None, :], lv, -1)
    masks = np.stack([[lvd == l for l in range(nlev + 1)] for lvd in (lvf, lvf.T)]).astype(np.float32)
    return e.reshape(2, (nmm - 1) * c, c), masks


def _level_sums(b, l, rev):
    c = b.shape[0]
    m = 1 << l
    parts = []
    for j in range(c // (2 * m)):
        lo = 2 * m * j
        left, right = b[lo:lo + m], b[lo + m:lo + 2 * m]
        if rev:
            r = b[lo + m:lo + m + 1]
            parts += [left - r, r - right]
        else:
            r = b[lo + m - 1:lo + m]
            parts += [r - left, right - r]
    return jnp.concatenate(parts, axis=0)


def _running_sums(la, rev):
    c = la.shape[0]
    row = lax.broadcasted_iota(jnp.int32, (c, 1), 0)
    b = la
    sh = 1
    while sh < c:
        if rev:
            b = b + jnp.where(row < c - sh, pltpu.roll(b, c - sh, 0), 0.0)
        else:
            b = b + jnp.where(row >= sh, pltpu.roll(b, sh, 0), 0.0)
        sh *= 2
    return b


def _scan_chunk(q, k, vs, la, st_refs, e, mk_ref, qmasks, rev):
    c, nlev, nmm = SCAN_CHUNK, SCAN_LEVELS, SCAN_MM_LEVELS
    b = _running_sums(la, rev)
    x = jnp.dot(e, la.astype(BF16), preferred_element_type=F32)
    odd = (lax.broadcasted_iota(jnp.int32, (c, 1), 0) & 1) == 1
    x0 = jnp.where(~odd if rev else odd, la, 0.0)
    tot = b[0:1] if rev else b[c - 1:c]
    qms = [(q * qm).astype(BF16) for qm in qmasks]
    kb = k.astype(BF16)
    acc = [lax.dot_general(qm, kb, NT, preferred_element_type=F32).astype(BF16) * mk_ref[nlev]
           for qm in qms]
    for l in range(nlev):
        if l == 0:
            xl = x0
        elif l < nmm:
            xl = x[(l - 1) * c:l * c]
        else:
            xl = _level_sums(b, l, rev)
        fl = jnp.exp(xl).astype(BF16)
        ksl = kb * fl
        al = lax.dot_general(jnp.concatenate([qm * fl for qm in qms], axis=0), ksl, NT,
                             preferred_element_type=F32)
        for h in range(len(qms)):
            acc[h] = acc[h] + al[h * c:(h + 1) * c].astype(BF16) * mk_ref[l]
    fq = jnp.exp(b).astype(BF16)
    ksk = kb * jnp.exp(tot - b).astype(BF16)
    decay = jnp.exp(tot)
    outs = []
    for h, qm in enumerate(qms):
        st = st_refs[h][...]
        o = jnp.dot(acc[h], vs[h], preferred_element_type=F32)
        o = o + lax.dot_general(qm * fq, st.astype(BF16), NT, preferred_element_type=F32)
        st_refs[h][...] = st * decay + lax.dot_general(vs[h], ksk, TN, preferred_element_type=F32)
        outs.append(o)
    return outs


def _scan_driver(prologue, qmasks, g_ref, nw_ref, e_ref, mk_ref, o_ref, of_ref, ob_ref, st_ref):
    c = SCAN_CHUNK
    s = of_ref.shape[0]
    nc = s // c
    nh = len(qmasks)
    st_ref[...] = jnp.zeros_like(st_ref)

    def one(r, d, dst_ref):
        q, k, v, la = prologue(r, d)
        vs = [v[:, h * LANES:(h + 1) * LANES] for h in range(nh)]
        outs = _scan_chunk(q, k, vs, la, [st_ref.at[d * nh + h] for h in range(nh)],
                           e_ref[d], mk_ref.at[d], qmasks, bool(d))
        for h in range(nh):
            dst_ref[pl.ds(r, c), h * LANES:(h + 1) * LANES] = outs[h]

    def body(n, carry):
        one(pl.multiple_of(n * c, c), 0, of_ref)
        one(pl.multiple_of((nc - 1 - n) * c, c), 1, ob_ref)
        return carry

    lax.fori_loop(0, nc, body, 0, unroll=2)

    def finish(n, carry):
        r = pl.multiple_of(n * c, c)
        for h in range(nh):
            cols = slice(h * LANES, (h + 1) * LANES)
            o = of_ref[pl.ds(r, c), cols] + ob_ref[pl.ds(r, c), cols]
            g = g_ref[0, pl.ds(r, c), cols]
            o_ref[0, pl.ds(r, c), cols] = (_rms(o, nw_ref[...]) * (g * _sigmoid(g))).astype(o_ref.dtype)
        return carry

    lax.fori_loop(0, nc, finish, 0)


def _gla_kernel(q_ref, k_ref, v_ref, g_ref, lr_ref, gw_ref, gb_ref, nw_ref, e_ref, mk_ref,
                o_ref, of_ref, ob_ref, st_ref, la_ref):
    c = SCAN_CHUNK
    lane = lax.broadcasted_iota(jnp.int32, (1, LANES), 1)
    qmasks = [jnp.where(lane // GLA_DK == h, GLA_DK ** -0.5, 0.0).astype(F32)
              for h in range(LANES // GLA_DK)]
    lr = lr_ref[0].astype(BF16)
    for d in range(2):
        gk = jnp.dot(lr, gw_ref[d], preferred_element_type=F32) + gb_ref[d]
        la_ref[d] = (jnp.minimum(gk, 0.0) - jnp.log(1.0 + jnp.exp(-jnp.abs(gk)))) * (
            1.0 / GLA_GATE_NORMALIZER)

    def prologue(r, d):
        q = q_ref[0, pl.ds(r, c), :]
        k = k_ref[0, pl.ds(r, c), :]
        v = v_ref[0, pl.ds(r, c), :].astype(BF16)
        return q, k, v, la_ref[d, pl.ds(r, c), :]

    _scan_driver(prologue, qmasks, g_ref, nw_ref, e_ref, mk_ref, o_ref, of_ref, ob_ref, st_ref)


def _hgrn_kernel(q_ref, ff_ref, fb_ref, v_ref, g_ref, lb_ref, nw_ref, e_ref, mk_ref,
                 o_ref, of_ref, ob_ref, st_ref):
    c = SCAN_CHUNK
    qmasks = [jnp.full((1, LANES), HG_DK ** -0.5, F32)]

    def prologue(r, d):
        q = q_ref[0, pl.ds(r, c), :]
        q = q * _sigmoid(q)
        z = (fb_ref if d else ff_ref)[0, pl.ds(r, c), :]
        lb = lb_ref[d]
        f = jnp.clip(lb + (1.0 - lb) * _sigmoid(z), 1e-6, 1.0)
        v = v_ref[0, pl.ds(r, c), :].astype(BF16)
        return q, 1.0 - f, v, jnp.log(f)

    _scan_driver(prologue, qmasks, g_ref, nw_ref, e_ref, mk_ref, o_ref, of_ref, ob_ref, st_ref)


def _scan_call(kernel, name, proj, cols, extra, extra_specs, norm_w, steps, heads_per_step,
               more_scratch=()):
    b, s, _ = proj.shape
    e_np, mk_np = _scan_constants(SCAN_CHUNK, SCAN_LEVELS, SCAN_MM_LEVELS)
    e = jnp.asarray(e_np, BF16)
    mk = jnp.asarray(mk_np, BF16)
    ow = heads_per_step * LANES

    def col_spec(off, width):
        return pl.BlockSpec((1, s, width), lambda bi, h: (bi, 0, off // width + h))

    in_specs = [col_spec(off, width) for off, width in cols] + extra_specs + [
        pl.BlockSpec((1, LANES), lambda bi, h: (0, 0)),
        pl.BlockSpec(e.shape, lambda bi, h: (0, 0, 0)),
        pl.BlockSpec(mk.shape, lambda bi, h: (0, 0, 0, 0))]
    return pl.pallas_call(
        kernel,
        grid=(b, steps),
        in_specs=in_specs,
        out_specs=pl.BlockSpec((1, s, ow), lambda bi, h: (bi, 0, h)),
        out_shape=jax.ShapeDtypeStruct((b, s, steps * ow), BF16),
        scratch_shapes=[pltpu.VMEM((s, ow), F32), pltpu.VMEM((s, ow), F32),
                        pltpu.VMEM((2 * heads_per_step, LANES, LANES), F32)] + list(more_scratch),
        compiler_params=_params("parallel", "parallel"),
        name=name,
    )(*([proj] * len(cols)), *extra, norm_w.reshape(1, LANES), e, mk)


def gla_group(proj, gk_w, gk_b, norm_w):
    gw = jnp.zeros((2, LANES, GLA_HEADS * GLA_DK), F32)
    gw = gw.at[0, 0:GLA_RANK].set(gk_w[0]).at[1, GLA_RANK:2 * GLA_RANK].set(gk_w[1]).astype(BF16)
    gb = gk_b.reshape(2, 1, GLA_HEADS * GLA_DK)
    pair = LANES // GLA_DK
    cols = [(COL_AQ, LANES), (COL_AK, LANES), (COL_AV, pair * GLA_DV), (COL_AG, pair * GLA_DV)]
    extra_specs = [pl.BlockSpec((1, proj.shape[1], LANES), lambda bi, h: (bi, 0, COL_LR // LANES)),
                   pl.BlockSpec((2, LANES, LANES), lambda bi, h: (0, 0, h)),
                   pl.BlockSpec((2, 1, LANES), lambda bi, h: (0, 0, h))]
    return _scan_call(_gla_kernel, "gla_scan", proj, cols, [proj, gw, gb], extra_specs, norm_w,
                      GLA_HEADS // pair, pair,
                      more_scratch=[pltpu.VMEM((2, proj.shape[1], LANES), F32)])


def hgrn_group(proj, lb, norm_w):
    cols = [(COL_BQ, LANES), (COL_BFF, LANES), (COL_BFB, LANES), (COL_BV, LANES), (COL_BG, LANES)]
    extra_specs = [pl.BlockSpec((2, 1, LANES), lambda bi, h: (0, 0, h))]
    return _scan_call(_hgrn_kernel, "hgrn_scan", proj, cols, [lb.reshape(2, 1, HG_HEADS * HG_DK)],
                      extra_specs, norm_w, HG_HEADS, 1)


def _rope(x, cos, sin_hi, sin_lo):
    return (x * cos + pltpu.roll(x, LANES - ROT_DIM // 2, 1) * sin_hi
            + pltpu.roll(x, ROT_DIM // 2, 1) * sin_lo)


def _swa_kernel(sink_ref, q_ref, k_ref, v_ref, t_ref, o_ref, kr_ref, vb_ref):
    kv = pl.program_id(1)
    n = pl.program_id(2)
    blk = SWA_BLOCK
    s = k_ref.shape[1]
    nb = s // blk
    nq = q_ref.shape[1] // blk
    rows = SWA_GROUP * blk

    @pl.when(n == 0)
    def _():
        zero = jnp.zeros((blk, SWA_HD), BF16)
        for ref in (kr_ref, vb_ref):
            ref[0:blk, :] = zero
            ref[blk + s:2 * blk + s, :] = zero
        for j in range(nb):
            r = slice(j * blk, (j + 1) * blk)
            kr_ref[blk + j * blk:2 * blk + j * blk, :] = _rope(
                k_ref[0, r, :], t_ref[0, r, :], t_ref[1, r, :], t_ref[2, r, :]).astype(BF16)
            vb_ref[blk + j * blk:2 * blk + j * blk, :] = v_ref[0, r, :].astype(BF16)

    qi = lax.broadcasted_iota(jnp.int32, (rows, 3 * blk), 0) & (blk - 1)
    ki = lax.broadcasted_iota(jnp.int32, (rows, 3 * blk), 1)
    window_bias = jnp.where(jnp.abs(ki - blk - qi) <= SWA_WINDOW, 0.0, NEG_BIG)
    col_block = lax.broadcasted_iota(jnp.int32, (1, 3 * blk), 1) // blk
    head = lax.broadcasted_iota(jnp.int32, (rows, 1), 0) // blk
    sink = jnp.zeros((rows, 1), F32)
    for g in range(SWA_GROUP):
        sink = jnp.where(head == g, sink_ref[0, kv * SWA_GROUP + g], sink)
    for j in range(nq):
        qb = n * nq + j
        r0 = pl.multiple_of(qb * blk, blk)
        cos, sin_hi, sin_lo = (t_ref[i, pl.ds(r0, blk), :] * (SWA_HD ** -0.5) for i in range(3))
        q4 = jnp.concatenate(
            [_rope(q_ref[0, j * blk:(j + 1) * blk, g * SWA_HD:(g + 1) * SWA_HD], cos, sin_hi, sin_lo)
             for g in range(SWA_GROUP)], axis=0).astype(BF16)
        kb = kr_ref[pl.ds(r0, 3 * blk), :]
        vb = vb_ref[pl.ds(r0, 3 * blk), :]
        logits = lax.dot_general(q4, kb, NT, preferred_element_type=F32)
        off_seq = ((col_block == 0) & (qb == 0)) | ((col_block == 2) & (qb == nb - 1))
        logits = logits + window_bias + jnp.where(off_seq, NEG_BIG, 0.0)
        m = jnp.maximum(jnp.max(logits, axis=-1, keepdims=True), sink)
        p = jnp.exp(logits - m)
        denom = jnp.sum(p, axis=-1, keepdims=True) + jnp.exp(sink - m)
        o = jnp.dot((p / denom).astype(BF16), vb, preferred_element_type=F32)
        for g in range(SWA_GROUP):
            o_ref[0, j * blk:(j + 1) * blk, g * SWA_HD:(g + 1) * SWA_HD] = (
                o[g * blk:(g + 1) * blk].astype(o_ref.dtype))


def swa_group(proj, sink, rope_t):
    b, s, _ = proj.shape
    gw = SWA_GROUP * SWA_HD
    tq = SWA_Q_TILE

    def kv_spec(col):
        return pl.BlockSpec((1, s, SWA_HD), lambda bi, kv, n: (bi, 0, col // SWA_HD + kv))

    return pl.pallas_call(
        _swa_kernel,
        grid=(b, SWA_KV_HEADS, s // tq),
        in_specs=[pl.BlockSpec(memory_space=pltpu.SMEM),
                  pl.BlockSpec((1, tq, gw), lambda bi, kv, n: (bi, n, COL_CQ // gw + kv)),
                  kv_spec(COL_CK), kv_spec(COL_CV),
                  pl.BlockSpec(rope_t.shape, lambda bi, kv, n: (0, 0, 0))],
        out_specs=pl.BlockSpec((1, tq, gw), lambda bi, kv, n: (bi, n, kv)),
        out_shape=jax.ShapeDtypeStruct((b, s, SWA_HEADS * SWA_HD), BF16),
        scratch_shapes=[pltpu.VMEM((s + 2 * SWA_BLOCK, SWA_HD), BF16),
                        pltpu.VMEM((s + 2 * SWA_BLOCK, SWA_HD), BF16)],
        compiler_params=_params("parallel", "parallel", "arbitrary"),
        name="swa",
    )(sink.reshape(1, SWA_HEADS), proj, proj, proj, rope_t)


def rope_tables(s):
    half = ROT_DIM // 2
    inv = ROPE_THETA ** (-jnp.arange(0, ROT_DIM, 2, dtype=F32) / ROT_DIM)
    ang = jnp.arange(s, dtype=F32)[:, None] * inv[None, :]
    cos, sin = jnp.cos(ang), jnp.sin(ang)
    z = jnp.zeros((s, LANES - ROT_DIM), F32)
    zh = jnp.zeros((s, half), F32)
    return jnp.stack([jnp.concatenate([cos, cos, z + 1.0], axis=1),
                      jnp.concatenate([-sin, zh, z], axis=1),
                      jnp.concatenate([zh, sin, z], axis=1)])


def _post_mixer_kernel(oa_ref, ob_ref, oc_ref, x_ref, wout_ref, nq_ref, wq_ref, kv_ref, wo_ref,
                       nf_ref, rw_ref, xo_ref, xn_ref, aff_ref):
    ka, kb = oa_ref.shape[-1], ob_ref.shape[-1]
    x = x_ref[0]
    x = x + jnp.dot(oa_ref[0], wout_ref[0:ka, :], preferred_element_type=F32)
    x = x + jnp.dot(ob_ref[0], wout_ref[ka:ka + kb, :], preferred_element_type=F32)
    x = x + jnp.dot(oc_ref[0], wout_ref[ka + kb:, :], preferred_element_type=F32)
    q = jnp.dot(_rms(x, nq_ref[...]).astype(BF16), wq_ref[...], preferred_element_type=F32).astype(BF16)
    w = MEM_HEADS * MEM_HD
    outs = []
    for h in range(MEM_HEADS):
        k = kv_ref[0, :, h * MEM_HD:(h + 1) * MEM_HD]
        v = kv_ref[0, :, w + h * MEM_HD:w + (h + 1) * MEM_HD]
        logits = lax.dot_general(q[:, h * MEM_HD:(h + 1) * MEM_HD], k, NT,
                                 preferred_element_type=F32) * (MEM_HD ** -0.5)
        p = jnp.exp(logits - jnp.max(logits, axis=-1, keepdims=True))
        o = jnp.dot(p.astype(BF16), v, preferred_element_type=F32) / jnp.sum(p, axis=-1, keepdims=True)
        outs.append(o.astype(BF16))
    x = x + jnp.dot(jnp.concatenate(outs, axis=-1), wo_ref[...], preferred_element_type=F32)
    xo_ref[0] = x
    xn = _rms(x, nf_ref[...]).astype(BF16)
    xn_ref[0] = xn
    logits = lax.dot_general(rw_ref[...], xn, NT, preferred_element_type=F32)
    expert = lax.broadcasted_iota(jnp.int32, logits.shape, 0)
    logits = jnp.where(expert < N_EXPERTS, logits, NEG_BIG)
    p = jnp.exp(logits - jnp.max(logits, axis=0, keepdims=True))
    aff_ref[0] = (p / jnp.sum(p, axis=0, keepdims=True))[0:N_EXPERTS]


def post_mixer(oa, ob, oc, x, w_out, nq, wq, kv, wo, nf, rw, layer, tm):
    b, s, d = x.shape
    n_mem, kvw = kv.shape[2], kv.shape[3]

    def rows(width):
        return pl.BlockSpec((1, tm, width), lambda bi, i: (bi, i, 0))

    def whole(a):
        return pl.BlockSpec((None,) + a.shape[1:], lambda bi, i: (layer, 0, 0),
                            pipeline_mode=pl.Buffered(1))

    vec = pl.BlockSpec((1, d), lambda bi, i: (0, 0))
    return pl.pallas_call(
        _post_mixer_kernel,
        grid=(b, s // tm),
        in_specs=[rows(oa.shape[2]), rows(ob.shape[2]), rows(oc.shape[2]), rows(d), whole(w_out),
                  vec, whole(wq),
                  pl.BlockSpec((None, 1, n_mem, kvw), lambda bi, i: (layer, bi, 0, 0)), whole(wo),
                  vec, whole(rw)],
        out_specs=[rows(d), rows(d), pl.BlockSpec((1, N_EXPERTS, tm), lambda bi, i: (bi, 0, i))],
        out_shape=[jax.ShapeDtypeStruct((b, s, d), F32), jax.ShapeDtypeStruct((b, s, d), BF16),
                   jax.ShapeDtypeStruct((b, N_EXPERTS, s), F32)],
        compiler_params=_params("parallel", "parallel"),
        name="post_mixer",
    )(oa, ob, oc, x, w_out, nq.reshape(1, d), wq, kv, wo, nf.reshape(1, d), rw)


def _select_kernel(aff_ref, tri_ref, slot_ref, *, cap):
    aff = aff_ref[...]
    key = pltpu.bitcast(aff, jnp.int32)

    def bit_step(i, lo):
        cand = lo | (1 << (30 - i))
        cnt = jnp.sum(jnp.where(key >= cand, 1.0, 0.0), axis=-1, keepdims=True)
        return jnp.where(cnt >= cap, cand, lo)

    thr = lax.fori_loop(0, 31, bit_step, jnp.zeros((aff.shape[0], 1), jnp.int32))
    gt = key > thr
    eq = key == thr
    need = cap - jnp.sum(jnp.where(gt, 1.0, 0.0), axis=-1, keepdims=True)
    tri = tri_ref[...]
    eq_rank = jnp.dot(jnp.where(eq, 1.0, 0.0).astype(BF16), tri, preferred_element_type=F32)
    sel = gt | (eq & (eq_rank <= need))
    pos = jnp.dot(jnp.where(sel, 1.0, 0.0).astype(BF16), tri, preferred_element_type=F32)
    slot_ref[...] = jnp.where(sel, pos.astype(jnp.int32) - 1, -1)


def select(aff_t, cap):
    r, s = aff_t.shape
    tri = jnp.triu(jnp.ones((s, s), BF16))
    return pl.pallas_call(
        functools.partial(_select_kernel, cap=cap),
        grid=(1,),
        in_specs=[pl.BlockSpec((r, s), lambda i: (0, 0)), pl.BlockSpec((s, s), lambda i: (0, 0))],
        out_specs=pl.BlockSpec((r, s), lambda i: (0, 0)),
        out_shape=jax.ShapeDtypeStruct((r, s), jnp.int32),
        compiler_params=_params("arbitrary"),
        name="select",
    )(aff_t, tri)


def _gather_kernel(slot_ref, aff_ref, xn_ref, xe_ref, gate_ref, *, cap):
    slot = slot_ref[0]
    hit = lax.broadcasted_iota(jnp.int32, (cap, slot.shape[1]), 0) == slot
    xe_ref[0, 0] = jnp.dot(jnp.where(hit, 1.0, 0.0).astype(BF16), xn_ref[0],
                           preferred_element_type=F32).astype(BF16)
    gate_ref[0, 0] = jnp.sum(jnp.where(hit, aff_ref[0], 0.0), axis=-1, keepdims=True)


def gather(slot_t, aff_t, xn, cap):
    b, s, d = xn.shape
    ne = slot_t.shape[0] // b
    return pl.pallas_call(
        functools.partial(_gather_kernel, cap=cap),
        grid=(b, ne),
        in_specs=[pl.BlockSpec((1, 1, s), lambda bi, e: (bi * ne + e, 0, 0)),
                  pl.BlockSpec((1, 1, s), lambda bi, e: (bi * ne + e, 0, 0)),
                  pl.BlockSpec((1, s, d), lambda bi, e: (bi, 0, 0))],
        out_specs=[pl.BlockSpec((1, 1, cap, d), lambda bi, e: (bi, e, 0, 0)),
                   pl.BlockSpec((1, 1, cap, 1), lambda bi, e: (bi, e, 0, 0))],
        out_shape=[jax.ShapeDtypeStruct((b, ne, cap, d), BF16),
                   jax.ShapeDtypeStruct((b, ne, cap, 1), F32)],
        compiler_params=_params("parallel", "arbitrary"),
        name="moe_gather",
    )(slot_t.reshape(b * ne, 1, s), aff_t.reshape(b * ne, 1, s), xn)


def _ffn_kernel(xe_ref, gate_ref, wg_ref, wu_ref, wd_ref, y_ref, h_ref):
    st = pl.program_id(1)
    nf, rows, tf = h_ref.shape
    d = xe_ref.shape[-1]

    @pl.when(st < nf)
    def _():
        xe = xe_ref[...].reshape(rows, d)
        hg = jnp.dot(xe, wg_ref[0].astype(BF16), preferred_element_type=F32)
        hu = jnp.dot(xe, wu_ref[0].astype(BF16), preferred_element_type=F32)
        h_ref[st] = (hg * _sigmoid(hg) * hu * gate_ref[...].reshape(rows, 1)).astype(BF16)

    @pl.when(st >= nf)
    def _():
        y = jnp.dot(h_ref[0], wd_ref[0, 0:tf, :].astype(BF16), preferred_element_type=F32)
        for f in range(1, nf):
            y = y + jnp.dot(h_ref[f], wd_ref[0, f * tf:(f + 1) * tf, :].astype(BF16),
                            preferred_element_type=F32)
        y_ref[...] = y.reshape(y_ref.shape).astype(y_ref.dtype)


def expert_ffn(xe, gate, wg, wu, wd, layer, tf, tn):
    b, ne, cap, d = xe.shape
    dexp = wg.shape[3]
    nf = dexp // tf
    return pl.pallas_call(
        _ffn_kernel,
        grid=(ne, nf + d // tn),
        in_specs=[pl.BlockSpec((b, 1, cap, d), lambda e, st: (0, e, 0, 0)),
                  pl.BlockSpec((b, 1, cap, 1), lambda e, st: (0, e, 0, 0)),
                  pl.BlockSpec((None, 1, d, tf), lambda e, st: (layer, e, 0, jnp.minimum(st, nf - 1))),
                  pl.BlockSpec((None, 1, d, tf), lambda e, st: (layer, e, 0, jnp.minimum(st, nf - 1))),
                  pl.BlockSpec((None, 1, dexp, tn), lambda e, st: (layer, e, 0, jnp.maximum(st - nf, 0)))],
        out_specs=pl.BlockSpec((b, 1, cap, tn), lambda e, st: (0, e, 0, jnp.maximum(st - nf, 0))),
        out_shape=jax.ShapeDtypeStruct((b, ne, cap, d), BF16),
        scratch_shapes=[pltpu.VMEM((nf, b * cap, tf), BF16)],
        compiler_params=_params("parallel", "arbitrary"),
        name="expert_ffn",
    )(xe, gate, wg, wu, wd)


def _scatter_kernel(slot_ref, y_ref, x_ref, o_ref, *, cap):
    ne = y_ref.shape[1]
    s = x_ref.shape[1]
    lane = lax.broadcasted_iota(jnp.int32, (s, cap), 1)
    acc = x_ref[0]
    for e in range(ne):
        hit = slot_ref[0, :, e:e + 1] == lane
        acc = acc + jnp.dot(jnp.where(hit, 1.0, 0.0).astype(BF16), y_ref[0, e],
                            preferred_element_type=F32)
    o_ref[0] = acc


def scatter(slot, y, x, tn):
    b, s, d = x.shape
    ne, cap = y.shape[1], y.shape[2]
    return pl.pallas_call(
        functools.partial(_scatter_kernel, cap=cap),
        grid=(b, d // tn),
        in_specs=[pl.BlockSpec((1, s, ne), lambda bi, j: (bi, 0, 0)),
                  pl.BlockSpec((1, ne, cap, tn), lambda bi, j: (bi, 0, 0, j)),
                  pl.BlockSpec((1, s, tn), lambda bi, j: (bi, 0, j))],
        out_specs=pl.BlockSpec((1, s, tn), lambda bi, j: (bi, 0, j)),
        out_shape=jax.ShapeDtypeStruct((b, s, d), F32),
        compiler_params=_params("parallel", "parallel"),
        name="moe_scatter",
    )(slot, y, x)


def _norm_kernel(x_ref, nw_ref, o_ref):
    o_ref[...] = _rms(x_ref[...], nw_ref[...])


def final_norm(x, nw, tm):
    m, d = x.shape
    return pl.pallas_call(
        _norm_kernel,
        grid=(m // tm,),
        in_specs=[pl.BlockSpec((tm, d), lambda i: (i, 0)), pl.BlockSpec((1, d), lambda i: (0, 0))],
        out_specs=pl.BlockSpec((tm, d), lambda i: (i, 0)),
        out_shape=jax.ShapeDtypeStruct((m, d), F32),
        compiler_params=_params("parallel"),
        name="final_norm",
    )(x, nw.reshape(1, d))


def _reorder_kernel(main_ref, tail_ref, o_ref, *, n_head, n_body):
    j = pl.program_id(1)
    tr = o_ref.shape[0]
    lr = tail_ref.shape[0]

    @pl.when(j < n_head)
    def _():
        o_ref[...] = main_ref[...].astype(BF16)

    @pl.when((j >= n_head) & (j < n_head + n_body))
    def _():
        o_ref[0:tr - lr, :] = main_ref[lr:tr, :].astype(BF16)
        o_ref[tr - lr:tr, :] = tail_ref[...].astype(BF16)

    @pl.when(j == n_head + n_body)
    def _():
        o_ref[0:lr, :] = tail_ref[...].astype(BF16)
        o_ref[lr:tr, :] = jnp.zeros((tr - lr, o_ref.shape[1]), BF16)


def reorder_w_in(wt, tr):
    nl, rows, d = wt.shape
    lr = LR_END - LR_START
    assert rows == IN_COLS and LR_START % tr == 0 and (IN_COLS - LR_END) % tr == 0 and tr % lr == 0
    n_head, n_body = LR_START // tr, (IN_COLS - LR_END) // tr
    assert (n_head + n_body + 1) * tr == PROJ_COLS
    per = tr // lr
    return pl.pallas_call(
        functools.partial(_reorder_kernel, n_head=n_head, n_body=n_body),
        grid=(nl, PROJ_COLS // tr),
        in_specs=[pl.BlockSpec((None, tr, d), lambda l, j: (l, jnp.minimum(j, n_head + n_body - 1), 0)),
                  pl.BlockSpec((None, lr, d), lambda l, j: (
                      l, jnp.where(j < n_head + n_body, per * (j + 1), LR_START // lr), 0))],
        out_specs=pl.BlockSpec((None, tr, d), lambda l, j: (l, j, 0)),
        out_shape=jax.ShapeDtypeStruct((nl, PROJ_COLS, d), BF16),
        compiler_params=_params("parallel", "parallel"),
        name="reorder_w_in",
    )(wt, wt)


def kernel(x, mem, norm_mix, w_in, gla_gk_w, gla_gk_b, gla_norm, hgrn_lb_logits, hgrn_norm, swa_sink, w_out, norm_mem_q, norm_mem_kv, mem_wq, mem_wkv, mem_wo, norm_ffn, router_w, expert_w_gate, expert_w_up, expert_w_down, norm_final):
    b, s, d = x.shape
    depth = w_in.shape[0]
    n_mem = mem.shape[1]
    cap = EC_CAPACITY * s // N_EXPERTS
    rope_t = rope_tables(s)
    sm = jax.nn.softmax(hgrn_lb_logits.astype(F32), axis=0)
    lower_bounds = jnp.clip(jnp.cumsum(sm, axis=0) - sm[0], 0.0, 1.0)
    mem2 = mem.reshape(b * n_mem, d)
    w_in_r = reorder_w_in(jnp.swapaxes(w_in, 1, 2), REORDER_ROWS)
    w_out_b, wq_b, wkv_b, wo_b = (w.astype(BF16) for w in (w_out, mem_wq, mem_wkv, mem_wo))
    rw_b = jnp.pad(jnp.swapaxes(router_w, 1, 2), ((0, 0), (0, LANES - N_EXPERTS), (0, 0))).astype(BF16)
    kv = norm_matmul_layers(mem2, norm_mem_kv, wkv_b, KV_PROJ_TN, BF16).reshape(depth, b, n_mem, -1)
    for l in range(depth):
        proj = norm_matmul(x.reshape(b * s, d), norm_mix[l], w_in_r, l, IN_PROJ_TM, IN_PROJ_TN, F32,
                           w_rows=True)
        proj = proj.reshape(b, s, PROJ_COLS)
        o_a = gla_group(proj, gla_gk_w[l], gla_gk_b[l], gla_norm[l])
        o_b = hgrn_group(proj, lower_bounds[l], hgrn_norm[l])
        o_c = swa_group(proj, swa_sink[l], rope_t)
        x, xn, aff = post_mixer(o_a, o_b, o_c, x, w_out_b, norm_mem_q[l], wq_b, kv, wo_b,
                                norm_ffn[l], rw_b, l, POST_MIXER_TM)
        aff_t = aff.reshape(b * N_EXPERTS, s)
        slot_t = select(aff_t, cap)
        xe, gate = gather(slot_t, aff_t, xn, cap)
        y = expert_ffn(xe, gate, expert_w_gate, expert_w_up, expert_w_down, l, FFN_TF, FFN_TN)
        slot = jnp.swapaxes(slot_t.reshape(b, N_EXPERTS, s), 1, 2)
        x = scatter(slot, y, x, SCATTER_TN)
    return final_norm(x.reshape(b * s, d), norm_final, FINAL_NORM_TM).reshape(b, s, d)
```
